```python
import jax, jax.numpy as jnp
from jax import lax
import numpy as np

D_MODEL = 1024
BATCH = 2
SEQ = 8192
DEPTH = 1

CHUNK = 64
Q_BLOCK = 128
SB_HEADS = 8
SB_HD = 64
SB_DIM = SB_HEADS * SB_HD
ML_HEADS = 4
ML_HD = 128
ML_DIM = ML_HEADS * ML_HD
CONV_W = 4
MEM_LEN = 256
MEM_HEADS = 4
MEM_HD = 128
MEM_DIM = MEM_HEADS * MEM_HD
N_BRANCH = 3
N_KEYS = 128
N_EXPERTS = N_KEYS * N_KEYS
PEER_HEADS = 8
PEER_QDIM = 256
PEER_HALF = PEER_QDIM // 2
PEER_TOPK = 16
PEER_BLOCK = 128
EPS = 1e-6
IN_COLS = 3 * SB_DIM + 3 * ML_DIM + 2 * ML_HEADS + ML_DIM + MEM_DIM + N_BRANCH * D_MODEL

kernel_name = "hybrid_stickbreak_mlstm_memxattn_peer"


def _rmsnorm(x, g):
    xf = x.astype(jnp.float32)
    y = xf * lax.rsqrt(jnp.mean(xf * xf, axis=-1, keepdims=True) + EPS)
    return (y * g.astype(jnp.float32)).astype(x.dtype)


def _heads(t, n_heads, hd):
    b, s, _ = t.shape
    return t.reshape(b, s, n_heads, hd).transpose(0, 2, 1, 3)


def _merge(t):
    b, h, s, d = t.shape
    return t.transpose(0, 2, 1, 3).reshape(b, s, h * d)


def _split_points():
    sizes = [SB_DIM] * 3 + [ML_DIM] * 3 + [ML_HEADS, ML_HEADS, ML_DIM, MEM_DIM, N_BRANCH * D_MODEL]
    return [int(c) for c in np.cumsum(sizes)[:-1]]


def _causal_conv(x, w, b):
    s = x.shape[1]
    xp = jnp.pad(x, ((0, 0), (CONV_W - 1, 0), (0, 0)))
    y = b
    for k in range(CONV_W):
        y = y + w[k] * xp[:, k:k + s]
    return y


def _stick_breaking(q, k, v):
    b, h, s, d = q.shape
    nb = s // Q_BLOCK
    scale = d ** -0.5
    qb = q.astype(jnp.float32).reshape(b, h, nb, Q_BLOCK, d).transpose(2, 0, 1, 3, 4)
    starts = jnp.arange(nb, dtype=jnp.int32) * Q_BLOCK
    kf = k.astype(jnp.float32)
    vf = v.astype(jnp.float32)
    s_idx = jnp.arange(s, dtype=jnp.int32)

    def block(args):
        qi, start = args
        z = jnp.einsum('bhqd,bhkd->bhqk', qi, kf) * scale
        t_idx = start + jnp.arange(Q_BLOCK, dtype=jnp.int32)
        mask = s_idx[None, :] < t_idx[:, None]
        log_keep = jnp.where(mask, jax.nn.log_sigmoid(-z), 0.0)
        suffix = lax.cumsum(log_keep, axis=3, reverse=True) - log_keep
        a = jnp.where(mask, jnp.exp(jax.nn.log_sigmoid(z) + suffix), 0.0)
        return jnp.einsum('bhqk,bhkd->bhqd', a, vf)

    out = lax.map(block, (qb, starts))
    return out.transpose(1, 2, 0, 3, 4).reshape(b, h, s, d)


def _mlstm_step(carry, inp):
    c_st, n_st, m_st = carry
    q, k, v, ig, lf = inp
    l = q.shape[2]
    causal = jnp.tril(jnp.ones((l, l), dtype=bool))
    bcum = jnp.cumsum(lf, axis=-1)
    dmat = jnp.where(causal, bcum[..., :, None] - bcum[..., None, :] + ig[..., None, :], -jnp.inf)
    m_inter = bcum + m_st[..., None]
    m_t = jnp.maximum(m_inter, jnp.max(dmat, axis=-1))
    w = jnp.einsum('bhtd,bhsd->bhts', q, k) * jnp.exp(dmat - m_t[..., None])
    inter = jnp.exp(m_inter - m_t)
    num = jnp.einsum('bhts,bhse->bhte', w, v) + inter[..., None] * jnp.einsum('bhtd,bhde->bhte', q, c_st)
    den = jnp.sum(w, axis=-1) + inter * jnp.einsum('bhtd,bhd->bht', q, n_st)
    h = num / jnp.maximum(jnp.abs(den), jnp.exp(-m_t))[..., None]
    b_last = bcum[..., -1]
    g = b_last[..., None] - bcum + ig
    m_new = jnp.maximum(b_last + m_st, jnp.max(g, axis=-1))
    ws = jnp.exp(g - m_new[..., None])
    decay = jnp.exp(b_last + m_st - m_new)
    c_new = decay[..., None, None] * c_st + jnp.einsum('bhs,bhsd,bhse->bhde', ws, k, v)
    n_new = decay[..., None] * n_st + jnp.einsum('bhs,bhsd->bhd', ws, k)
    return (c_new, n_new, m_new), h


def _mlstm(q, k, v, ig, lf):
    b, h, s, d = q.shape
    nc = s // CHUNK

    def chunks(t):
        return jnp.moveaxis(t.reshape(b, h, nc, CHUNK, *t.shape[3:]), 2, 0)

    init = (jnp.zeros((b, h, d, d), jnp.float32), jnp.zeros((b, h, d), jnp.float32),
            jnp.zeros((b, h), jnp.float32))
    _, hs = lax.scan(_mlstm_step, init, (chunks(q), chunks(k), chunks(v), chunks(ig), chunks(lf)))
    return jnp.moveaxis(hs, 0, 2).reshape(b, h, s, d)


def _memory_attention(q, memn, w_kv, q_g, k_g):
    kv = memn @ w_kv
    km, vm = jnp.split(kv, 2, axis=-1)
    qh = _rmsnorm(_heads(q, MEM_HEADS, MEM_HD), q_g).astype(jnp.float32)
    kh = _rmsnorm(_heads(km, MEM_HEADS, MEM_HD), k_g).astype(jnp.float32)
    vh = _heads(vm, MEM_HEADS, MEM_HD).astype(jnp.float32)
    sc = jnp.einsum('bhqd,bhkd->bhqk', qh, kh) * (MEM_HD ** -0.5)
    p = jax.nn.softmax(sc, axis=-1)
    return _merge(jnp.einsum('bhqk,bhkd->bhqd', p, vh)).astype(q.dtype)


def _peer(xn, w_q, k1, k2, u_tab, v_tab):
    b, s, d = xn.shape
    t = b * s
    xf = xn.reshape(t, d)
    qry = (xf @ w_q).reshape(t, PEER_HEADS, 2, PEER_HALF).astype(jnp.float32)
    s1 = jnp.einsum('thd,hkd->thk', qry[:, :, 0], k1.astype(jnp.float32))
    s2 = jnp.einsum('thd,hkd->thk', qry[:, :, 1], k2.astype(jnp.float32))
    v1, i1 = lax.top_k(s1, PEER_TOPK)
    v2, i2 = lax.top_k(s2, PEER_TOPK)
    cand = (v1[..., :, None] + v2[..., None, :]).reshape(t, PEER_HEADS, PEER_TOPK * PEER_TOPK)
    top_v, top_c = lax.top_k(cand, PEER_TOPK)
    e1 = jnp.take_along_axis(i1, top_c // PEER_TOPK, axis=-1)
    e2 = jnp.take_along_axis(i2, top_c % PEER_TOPK, axis=-1)
    idx = e1 * N_KEYS + e2
    gates = jax.nn.softmax(top_v, axis=-1).astype(xn.dtype)
    nb = t // PEER_BLOCK

    def block(args):
        xb, ib, gb = args
        u = u_tab[ib]
        act = jax.nn.gelu(jnp.einsum('phkd,pd->phk', u, xb), approximate=False)
        return jnp.einsum('phk,phkd->pd', gb * act, v_tab[ib])

    out = lax.map(block, (xf.reshape(nb, PEER_BLOCK, d), idx.reshape(nb, PEER_BLOCK, PEER_HEADS, PEER_TOPK),
                          gates.reshape(nb, PEER_BLOCK, PEER_HEADS, PEER_TOPK)))
    return out.reshape(b, s, d)


def setup_inputs(seed: int = 0) -> dict:
    key = jax.random.key(seed)
    ks = jax.random.split(key, 26)

    def nrm(k, shape, scale):
        return jax.random.normal(k, shape, jnp.float32) * scale

    L = DEPTH
    return {
        "x": nrm(ks[0], (BATCH, SEQ, D_MODEL), 1.0),
        "mem": nrm(ks[1], (BATCH, MEM_LEN, D_MODEL), 1.0),
        "g_mix": 1.0 + nrm(ks[2], (L, D_MODEL), 0.1),
        "w_in": nrm(ks[3], (L, D_MODEL, IN_COLS), D_MODEL ** -0.5),
        "b_igate": nrm(ks[4], (L, ML_HEADS), 0.1),
        "b_fgate": 3.0 + nrm(ks[5], (L, ML_HEADS), 0.5),
        "conv_w": nrm(ks[6], (L, CONV_W, 2 * ML_DIM), 0.5),
        "conv_b": nrm(ks[7], (L, 2 * ML_DIM), 0.02),
        "ml_norm_g": 1.0 + nrm(ks[8], (L, ML_HD), 0.1),
        "mem_norm_g": 1.0 + nrm(ks[9], (L, D_MODEL), 0.1),
        "w_mem_kv": nrm(ks[10], (L, D_MODEL, 2 * MEM_DIM), D_MODEL ** -0.5),
        "q_norm_g": 1.0 + nrm(ks[11], (L, MEM_HD), 0.1),
        "k_norm_g": 1.0 + nrm(ks[12], (L, MEM_HD), 0.1),
        "w_sb_o": nrm(ks[13], (L, SB_DIM, D_MODEL), SB_DIM ** -0.5),
        "w_ml_o": nrm(ks[14], (L, ML_DIM, D_MODEL), ML_DIM ** -0.5),
        "w_mem_o": nrm(ks[15], (L, MEM_DIM, D_MODEL), MEM_DIM ** -0.5),
        "b_gate": nrm(ks[16], (L, N_BRANCH * D_MODEL), 0.1),
        "w_out": nrm(ks[17], (L, D_MODEL, D_MODEL), D_MODEL ** -0.5),
        "g_ffn": 1.0 + nrm(ks[18], (L, D_MODEL), 0.1),
        "w_peer_q": nrm(ks[19], (L, D_MODEL, PEER_HEADS * PEER_QDIM), D_MODEL ** -0.5),
        "peer_k1": nrm(ks[20], (L, PEER_HEADS, N_KEYS, PEER_HALF), PEER_HALF ** -0.5),
        "peer_k2": nrm(ks[21], (L, PEER_HEADS, N_KEYS, PEER_HALF), PEER_HALF ** -0.5),
        "peer_u": nrm(ks[22], (L, N_EXPERTS, D_MODEL), D_MODEL ** -0.5),
        "peer_v": nrm(ks[23], (L, N_EXPERTS, D_MODEL), 0.5),
    }


def reference(x, mem, g_mix, w_in, b_igate, b_fgate, conv_w, conv_b, ml_norm_g, mem_norm_g, w_mem_kv,
              q_norm_g, k_norm_g, w_sb_o, w_ml_o, w_mem_o, b_gate, w_out, g_ffn, w_peer_q, peer_k1,
              peer_k2, peer_u, peer_v):
    bsz, s, _ = x.shape
    h = x
    splits = _split_points()
    for l in range(DEPTH):
        xn = _rmsnorm(h, g_mix[l])
        proj = xn @ w_in[l]
        (sb_q, sb_k, sb_v, ml_q, ml_k, ml_v, ml_i, ml_f, ml_o, mem_q, gate_pre) = jnp.split(proj, splits, axis=-1)

        y_sb = _stick_breaking(_heads(sb_q, SB_HEADS, SB_HD), _heads(sb_k, SB_HEADS, SB_HD),
                               _heads(sb_v, SB_HEADS, SB_HD))
        y_sb = _merge(y_sb).astype(x.dtype)

        qk = jax.nn.silu(_causal_conv(jnp.concatenate([ml_q, ml_k], axis=-1), conv_w[l], conv_b[l]))
        mq, mk = jnp.split(qk, 2, axis=-1)
        qh = _heads(mq, ML_HEADS, ML_HD).astype(jnp.float32)
        kh = _heads(mk, ML_HEADS, ML_HD).astype(jnp.float32) * (ML_HD ** -0.5)
        vh = _heads(ml_v, ML_HEADS, ML_HD).astype(jnp.float32)
        ig = (ml_i + b_igate[l]).astype(jnp.float32).transpose(0, 2, 1)
        lf = jax.nn.log_sigmoid((ml_f + b_fgate[l]).astype(jnp.float32)).transpose(0, 2, 1)
        hm = _mlstm(qh, kh, vh, ig, lf)
        hm = _rmsnorm(hm, ml_norm_g[l])
        y_ml = (_merge(hm) * jax.nn.sigmoid(ml_o.astype(jnp.float32))).astype(x.dtype)

        memn = _rmsnorm(mem, mem_norm_g[l])
        y_mem = _memory_attention(mem_q, memn, w_mem_kv[l], q_norm_g[l], k_norm_g[l])

        gates = jax.nn.sigmoid(gate_pre + b_gate[l]).reshape(bsz, s, N_BRANCH, D_MODEL)
        merged = (gates[:, :, 0] * (y_sb @ w_sb_o[l]) + gates[:, :, 1] * (y_ml @ w_ml_o[l])
                  + gates[:, :, 2] * (y_mem @ w_mem_o[l]))
        h = h + merged @ w_out[l]

        hn = _rmsnorm(h, g_ffn[l])
        h = h + _peer(hn, w_peer_q[l], peer_k1[l], peer_k2[l], peer_u[l], peer_v[l])
    return h
```

```python
import functools

import jax
import jax.numpy as jnp
from jax import lax
from jax.experimental import pallas as pl
from jax.experimental.pallas import tpu as pltpu

F32 = jnp.float32
BF16 = jnp.bfloat16

EPS = 1e-6
SB_HEADS, SB_HD = 8, 64
ML_HEADS, ML_HD = 4, 128
MEM_HEADS, MEM_HD = 4, 128
CONV_W = 4
N_KEYS = 128
PEER_HEADS = 8
PEER_TOPK = 16
LANES = 128
VMEM_LIMIT = 56 << 20
NEG = -1e30
SB_DEAD = -104.0


def _cparams(sem):
    return pltpu.CompilerParams(dimension_semantics=sem, vmem_limit_bytes=VMEM_LIMIT)


def _rms(x, g):
    return x * lax.rsqrt(jnp.mean(x * x, axis=-1, keepdims=True) + EPS) * g


def _dot(a, b):
    return jnp.dot(a, b, preferred_element_type=F32)


def _dot_nt(a, b):
    return lax.dot_general(a, b, (((1,), (1,)), ((), ())), preferred_element_type=F32)


def _split_bf16(x):
    hi = x.astype(BF16)
    lo = (x - hi.astype(F32)).astype(BF16)
    return hi, lo


def _in_proj_kernel(x_ref, g_ref, w_ref, p1_ref, p2_ref, if_ref, *, n1, n2):
    xn = _rms(x_ref[...], g_ref[...]).astype(BF16)
    step = 1024
    for c in range(0, n1, step):
        p1_ref[:, c:c + step] = _dot(xn, w_ref[:, c:c + step]).astype(BF16)
    for c in range(0, n2, step):
        p2_ref[:, c:c + step] = _dot(xn, w_ref[:, n1 + c:n1 + c + step]).astype(BF16)
    if_ref[...] = _dot(xn, w_ref[:, n1 + n2:])


def _in_proj(x2, g, w_all, n1, n2, tm):
    t, d = x2.shape
    ncol = w_all.shape[1]
    return pl.pallas_call(
        functools.partial(_in_proj_kernel, n1=n1, n2=n2),
        grid=(t // tm,),
        in_specs=[
            pl.BlockSpec((tm, d), lambda i: (i, 0)),
            pl.BlockSpec((1, d), lambda i: (0, 0)),
            pl.BlockSpec((d, ncol), lambda i: (0, 0)),
        ],
        out_specs=[
            pl.BlockSpec((tm, n1), lambda i: (i, 0)),
            pl.BlockSpec((tm, n2), lambda i: (i, 0)),
            pl.BlockSpec((tm, LANES), lambda i: (i, 0)),
        ],
        out_shape=[
            jax.ShapeDtypeStruct((t, n1), BF16),
            jax.ShapeDtypeStruct((t, n2), BF16),
            jax.ShapeDtypeStruct((t, LANES), F32),
        ],
        compiler_params=_cparams(("arbitrary",)),
        name="in_proj",
    )(x2, g, w_all)


def _sb_kernel(q_ref, k_ref, v_ref, o_ref, *, tb, scale):
    i = pl.program_id(2)
    q = q_ref[...]
    lane = lax.broadcasted_iota(jnp.int32, (tb, LANES), 1)
    first = lane < SB_HD
    zero = jnp.zeros_like(q)
    qs = (jnp.where(first, q, zero), jnp.where(first, zero, q))
    row = lax.broadcasted_iota(jnp.int32, (tb, tb), 0)
    col = lax.broadcasted_iota(jnp.int32, (tb, tb), 1)
    below = row > col
    ones_below = jnp.where(below, 1.0, 0.0).astype(BF16)

    def block(j, cs, acc, diag):
        k = k_ref[pl.ds(pl.multiple_of(j * tb, tb), tb), :]
        v = v_ref[pl.ds(pl.multiple_of(j * tb, tb), tb), :]
        outs, new_cs = [], []
        for qh, c in zip(qs, cs):
            z = _dot_nt(qh, k) * scale
            lk = -(jnp.maximum(z, 0.0) + jnp.log1p(jnp.exp(-jnp.abs(z))))
            if diag:
                lk = jnp.where(below, lk, 0.0)
            hi, lo = _split_bf16(lk)
            suffix = _dot(hi, ones_below) + _dot(lo, ones_below)
            a = jnp.exp(z + lk + suffix + c)
            if diag:
                a = jnp.where(below, a, 0.0)
            outs.append(_dot(a.astype(BF16), v))
            new_cs.append(c + jnp.sum(lk, axis=1, keepdims=True))
        acc = acc + jnp.where(first, outs[0], outs[1])
        return tuple(new_cs), acc

    c0 = jnp.zeros((tb, 1), F32)
    cs, acc = block(i, (c0, c0), jnp.zeros((tb, LANES), F32), True)

    def alive(carry):
        j, cs, _ = carry
        return jnp.logical_and(j >= 0, jnp.max(jnp.maximum(cs[0], cs[1])) > SB_DEAD)

    def body(carry):
        j, cs, acc = carry
        cs, acc = block(j, cs, acc, False)
        return j - 1, cs, acc

    _, _, acc = lax.while_loop(alive, body, (i - 1, cs, acc))
    o_ref[...] = acc.astype(o_ref.dtype)


def _stick_break(p1, bsz, s, tb):
    t = bsz * s
    nq = s // tb
    npair = SB_HEADS * SB_HD // LANES
    return pl.pallas_call(
        functools.partial(_sb_kernel, tb=tb, scale=SB_HD ** -0.5),
        grid=(bsz, npair, nq),
        in_specs=[
            pl.BlockSpec((tb, LANES), lambda b, p, i: (b * nq + i, p)),
            pl.BlockSpec((s, LANES), lambda b, p, i: (b, npair + p)),
            pl.BlockSpec((s, LANES), lambda b, p, i: (b, 2 * npair + p)),
        ],
        out_specs=pl.BlockSpec((tb, LANES), lambda b, p, i: (b * nq + i, p)),
        out_shape=jax.ShapeDtypeStruct((t, npair * LANES), BF16),
        compiler_params=_cparams(("arbitrary", "arbitrary", "arbitrary")),
        name="stick_break",
    )(p1, p1, p1)


def _mlstm_kernel(qr_ref, kr_ref, v_ref, qp_ref, kp_ref, if_ref, o_ref, cw_ref, cb_ref, bif_ref, ng_ref,
                  y_ref, qpad, kpad, c_st, n_st, m_st, *, L):
    ch = pl.program_id(1)
    dm = ML_HEADS * ML_HD

    @pl.when(ch == 0)
    def _():
        c_st[...] = jnp.zeros_like(c_st)
        n_st[...] = jnp.zeros_like(n_st)
        m_st[...] = jnp.zeros_like(m_st)

    keep_prev = jnp.where(ch == 0, 0.0, 1.0)
    convd = []
    for idx, (cur, prev, pad) in enumerate(((qr_ref, qp_ref, qpad), (kr_ref, kp_ref, kpad))):
        pad[0:8, :] = prev[...].astype(F32) * keep_prev
        pad[8:8 + L, :] = cur[...].astype(F32)
        y = cb_ref[:, idx * dm:(idx + 1) * dm]
        for tap in range(CONV_W):
            y = y + cw_ref[tap:tap + 1, idx * dm:(idx + 1) * dm] * pad[pl.ds(8 - (CONV_W - 1) + tap, L), :]
        convd.append(y * jax.nn.sigmoid(y))
    q_all = convd[0]
    k_all = convd[1] * (ML_HD ** -0.5)

    gates = if_ref[...] + bif_ref[...]
    lf = jnp.minimum(gates, 0.0) - jnp.log1p(jnp.exp(-jnp.abs(gates)))
    row = lax.broadcasted_iota(jnp.int32, (L, L), 0)
    col = lax.broadcasted_iota(jnp.int32, (L, L), 1)
    causal = col <= row
    tri = jnp.where(causal, 1.0, 0.0).astype(BF16)
    hi, lo = _split_bf16(lf)
    lo2 = (lf - hi.astype(F32) - lo.astype(F32)).astype(BF16)
    bcum = _dot(tri, hi) + _dot(tri, lo) + _dot(tri, lo2)
    lane = lax.broadcasted_iota(jnp.int32, (L, LANES), 1)
    packed = jnp.where(lane < ML_HEADS, gates, bcum)
    packed_t = packed.T

    for h in range(ML_HEADS):
        sl = slice(h * ML_HD, (h + 1) * ML_HD)
        q = q_all[:, sl]
        k = k_all[:, sl]
        v = v_ref[:, sl]
        qb = q.astype(BF16)
        ig_col = gates[:, h:h + 1]
        bc_col = bcum[:, ML_HEADS + h:ML_HEADS + h + 1]
        ig_row = packed_t[h:h + 1, :]
        bc_row = packed_t[ML_HEADS + h:ML_HEADS + h + 1, :]
        m_prev = m_st[h][0:1, 0:1]

        dmat = jnp.where(causal, bc_col - bc_row + ig_row, NEG)
        m_inter = bc_col + m_prev
        m_t = jnp.maximum(m_inter, jnp.max(dmat, axis=1, keepdims=True))
        w = _dot_nt(qb, k.astype(BF16)) * jnp.exp(dmat - m_t)
        inter = jnp.exp(m_inter - m_t)
        num = _dot(w.astype(BF16), v) + inter * _dot(qb, c_st[h].astype(BF16))
        den = jnp.sum(w, axis=1, keepdims=True) + inter * jnp.sum(q * n_st[h], axis=1, keepdims=True)
        hv = num / jnp.maximum(jnp.abs(den), jnp.exp(-m_t))

        b_last = bc_col[L - 1:L, :]
        g_col = b_last - bc_col + ig_col
        m_new = jnp.maximum(b_last + m_prev, jnp.max(g_col, axis=0, keepdims=True))
        ws_col = jnp.exp(g_col - m_new)
        decay = jnp.exp(b_last + m_prev - m_new)
        kw = k * ws_col
        c_st[h] = decay * c_st[h] + _dot(kw.T.astype(BF16), v)
        n_st[h] = decay * n_st[h] + jnp.sum(kw, axis=0, keepdims=True)
        m_st[h] = jnp.broadcast_to(m_new, (8, LANES))

        hn = _rms(hv, ng_ref[...])
        y_ref[:, sl] = (hn * jax.nn.sigmoid(o_ref[:, sl].astype(F32))).astype(y_ref.dtype)


def _mlstm(p1, p2, gates_if, conv_w, conv_b, bias_if, norm_g, bsz, s, L):
    t = bsz * s
    nc = s // L
    dm = ML_HEADS * ML_HD
    lb = L // 8

    def prev_map(b, c):
        return (jnp.maximum((b * nc + c) * lb - 1, 0), 0)

    return pl.pallas_call(
        functools.partial(_mlstm_kernel, L=L),
        grid=(bsz, nc),
        in_specs=[
            pl.BlockSpec((L, dm), lambda b, c: (b * nc + c, 3)),
            pl.BlockSpec((L, dm), lambda b, c: (b * nc + c, 4)),
            pl.BlockSpec((L, dm), lambda b, c: (b * nc + c, 5)),
            pl.BlockSpec((8, dm), lambda b, c: (prev_map(b, c)[0], 3)),
            pl.BlockSpec((8, dm), lambda b, c: (prev_map(b, c)[0], 4)),
            pl.BlockSpec((L, LANES), lambda b, c: (b * nc + c, 0)),
            pl.BlockSpec((L, dm), lambda b, c: (b * nc + c, 0)),
            pl.BlockSpec((CONV_W, 2 * dm), lambda b, c: (0, 0)),
            pl.BlockSpec((1, 2 * dm), lambda b, c: (0, 0)),
            pl.BlockSpec((1, LANES), lambda b, c: (0, 0)),
            pl.BlockSpec((1, ML_HD), lambda b, c: (0, 0)),
        ],
        out_specs=pl.BlockSpec((L, dm), lambda b, c: (b * nc + c, 0)),
        out_shape=jax.ShapeDtypeStruct((t, dm), BF16),
        scratch_shapes=[
            pltpu.VMEM((L + 8, dm), F32),
            pltpu.VMEM((L + 8, dm), F32),
            pltpu.VMEM((ML_HEADS, ML_HD, ML_HD), F32),
            pltpu.VMEM((ML_HEADS, 1, ML_HD), F32),
            pltpu.VMEM((ML_HEADS, 8, LANES), F32),
        ],
        compiler_params=_cparams(("arbitrary", "arbitrary")),
        name="mlstm",
    )(p1, p1, p1, p1, p1, gates_if, p2, conv_w, conv_b, bias_if, norm_g)


def _mem_kv_kernel(mem_ref, g_ref, w_ref, kg_ref, k_ref, v_ref):
    memn = _rms(mem_ref[0], g_ref[...]).astype(BF16)
    kv = _dot(memn, w_ref[...])
    dm = MEM_HEADS * MEM_HD
    for h in range(MEM_HEADS):
        sl = slice(h * MEM_HD, (h + 1) * MEM_HD)
        k_ref[0, :, sl] = _rms(kv[:, sl], kg_ref[...]).astype(BF16)
    v_ref[0] = kv[:, dm:].astype(BF16)


def _mem_kv(mem, g, w_kv, k_g):
    bsz, ml, d = mem.shape
    dm = MEM_HEADS * MEM_HD
    return pl.pallas_call(
        _mem_kv_kernel,
        grid=(bsz,),
        in_specs=[
            pl.BlockSpec((1, ml, d), lambda b: (b, 0, 0)),
            pl.BlockSpec((1, d), lambda b: (0, 0)),
            pl.BlockSpec((d, 2 * dm), lambda b: (0, 0)),
            pl.BlockSpec((1, MEM_HD), lambda b: (0, 0)),
        ],
        out_specs=[
            pl.BlockSpec((1, ml, dm), lambda b: (b, 0, 0)),
            pl.BlockSpec((1, ml, dm), lambda b: (b, 0, 0)),
        ],
        out_shape=[jax.ShapeDtypeStruct((bsz, ml, dm), BF16)] * 2,
        compiler_params=_cparams(("arbitrary",)),
        name="mem_kv",
    )(mem, g, w_kv, k_g)


def _merge_kernel(ysb_ref, yml_ref, mq_ref, g0_ref, g1_ref, g2_ref, x_ref, kh_ref, vh_ref,
                  wsb_ref, wml_ref, wmem_ref, wout_ref, bg_ref, qg_ref, gf_ref, h_ref, hnt_ref):
    d = x_ref.shape[1]
    heads = []
    for h in range(MEM_HEADS):
        sl = slice(h * MEM_HD, (h + 1) * MEM_HD)
        qh = _rms(mq_ref[:, sl].astype(F32), qg_ref[...]).astype(BF16)
        sc = _dot_nt(qh, kh_ref[0, :, sl]) * (MEM_HD ** -0.5)
        p = jnp.exp(sc - jnp.max(sc, axis=1, keepdims=True))
        p = p / jnp.sum(p, axis=1, keepdims=True)
        heads.append(_dot(p.astype(BF16), vh_ref[0, :, sl]))
    ymem = jnp.concatenate(heads, axis=1).astype(BF16)

    def gate(ref, idx):
        return jax.nn.sigmoid(ref[...].astype(F32) + bg_ref[:, idx * d:(idx + 1) * d])

    merged = (gate(g0_ref, 0) * _dot(ysb_ref[...], wsb_ref[...])
              + gate(g1_ref, 1) * _dot(yml_ref[...], wml_ref[...])
              + gate(g2_ref, 2) * _dot(ymem, wmem_ref[...]))
    h1 = x_ref[...] + _dot(merged.astype(BF16), wout_ref[...])
    h_ref[...] = h1
    hnt_ref[...] = _rms(h1, gf_ref[...]).T.astype(BF16)


def _merge(ysb, yml, p2, x2, kh, vh, wsb, wml, wmem, wout, b_gate, q_g, g_ffn, s, tm):
    t, d = x2.shape
    dm = MEM_HEADS * MEM_HD
    ml = kh.shape[1]
    per_b = s // tm
    const = lambda i: (0, 0)
    return pl.pallas_call(
        _merge_kernel,
        grid=(t // tm,),
        in_specs=[
            pl.BlockSpec((tm, dm), lambda i: (i, 0)),
            pl.BlockSpec((tm, dm), lambda i: (i, 0)),
            pl.BlockSpec((tm, dm), lambda i: (i, 1)),
            pl.BlockSpec((tm, d), lambda i: (i, 1)),
            pl.BlockSpec((tm, d), lambda i: (i, 2)),
            pl.BlockSpec((tm, d), lambda i: (i, 3)),
            pl.BlockSpec((tm, d), lambda i: (i, 0)),
            pl.BlockSpec((1, ml, dm), lambda i: (i // per_b, 0, 0)),
            pl.BlockSpec((1, ml, dm), lambda i: (i // per_b, 0, 0)),
            pl.BlockSpec((dm, d), const),
            pl.BlockSpec((dm, d), const),
            pl.BlockSpec((dm, d), const),
            pl.BlockSpec((d, d), const),
            pl.BlockSpec((1, 3 * d), const),
            pl.BlockSpec((1, MEM_HD), const),
            pl.BlockSpec((1, d), const),
        ],
        out_specs=[
            pl.BlockSpec((tm, d), lambda i: (i, 0)),
            pl.BlockSpec((d, tm), lambda i: (0, i)),
        ],
        out_shape=[jax.ShapeDtypeStruct((t, d), F32), jax.ShapeDtypeStruct((d, t), BF16)],
        compiler_params=_cparams(("arbitrary",)),
        name="merge",
    )(ysb, yml, p2, p2, p2, p2, x2, kh, vh, wsb, wml, wmem, wout, b_gate, q_g, g_ffn)


def _top16(x):
    vals = []
    for _ in range(PEER_TOPK):
        m = jnp.max(x, axis=0, keepdims=True)
        vals.append(m)
        x = jnp.where(x == m, NEG, x)
    return vals


_PAIRS = [(a, b) for a in range(PEER_TOPK) for b in range(PEER_TOPK) if (a + 1) * (b + 1) <= PEER_TOPK]
_N_CAND = -(-len(_PAIRS) // 8) * 8


def _route_kernel(hnt_ref, wq_ref, keys_ref, s2_ref, e2_ref, r_ref, e1_ref, cand_ref):
    qt = _dot(wq_ref[...], hnt_ref[...]).astype(BF16)
    cand_ref[...] = jnp.full(cand_ref.shape, NEG, F32)
    for h in range(PEER_HEADS):
        s1 = _dot(keys_ref[2 * h], qt[(2 * h) * N_KEYS:(2 * h + 1) * N_KEYS, :])
        s2 = _dot(keys_ref[2 * h + 1], qt[(2 * h + 1) * N_KEYS:(2 * h + 2) * N_KEYS, :])
        v1 = _top16(s1)
        v2 = _top16(s2)
        for n, (a, b) in enumerate(_PAIRS):
            cand_ref[n:n + 1, :] = v1[a] + v2[b]
        top = _top16(cand_ref[...])
        thr = top[PEER_TOPK - 1]
        zsum = jnp.ones_like(thr)
        for r in range(1, PEER_TOPK):
            zsum = zsum + jnp.exp(top[r] - top[0])
        s2_ref[h] = s2
        e2_ref[h] = jnp.exp(s2 - v2[0])
        r_ref[h] = thr - s1
        e1_ref[h] = jnp.exp(s1 - v1[0]) / zsum


def _peer_route(hnt, wq_t, keys, tm):
    d, t = hnt.shape
    nq = wq_t.shape[0]
    blk = pl.BlockSpec((PEER_HEADS, N_KEYS, tm), lambda i: (0, 0, i))
    shp = jax.ShapeDtypeStruct((PEER_HEADS, N_KEYS, t), F32)
    return pl.pallas_call(
        _route_kernel,
        grid=(t // tm,),
        in_specs=[
            pl.BlockSpec((d, tm), lambda i: (0, i)),
            pl.BlockSpec((nq, d), lambda i: (0, 0)),
            pl.BlockSpec((2 * PEER_HEADS, N_KEYS, N_KEYS), lambda i: (0, 0, 0)),
        ],
        out_specs=[blk] * 4,
        out_shape=[shp] * 4,
        scratch_shapes=[pltpu.VMEM((_N_CAND, tm), F32)],
        compiler_params=_cparams(("arbitrary",)),
        name="peer_route",
    )(hnt, wq_t, keys)


def _dense_kernel(hnt_ref, u_ref, vt_ref, s2_ref, e2_ref, r_ref, e1_ref, h_ref, o_ref, acc_ref, *, rows):
    e = pl.program_id(1)

    @pl.when(e == 0)
    def _():
        acc_ref[...] = jnp.zeros_like(acc_ref)

    act_t = _dot(u_ref[...], hnt_ref[...])
    parts = []
    for ii in range(rows):
        i = e * rows + ii
        a = act_t[ii * N_KEYS:(ii + 1) * N_KEYS, :]
        gate = jnp.zeros_like(a)
        for h in range(PEER_HEADS):
            sel = s2_ref[h] >= r_ref[h, pl.ds(i, 1), :]
            gate = gate + jnp.where(sel, e2_ref[h] * e1_ref[h, pl.ds(i, 1), :], 0.0)
        gelu = 0.5 * a * (1.0 + lax.erf(a * (2.0 ** -0.5)))
        parts.append((gelu * gate).astype(BF16))
    acc_ref[...] += _dot(vt_ref[...], jnp.concatenate(parts, axis=0))

    @pl.when(e == pl.num_programs(1) - 1)
    def _():
        o_ref[...] = h_ref[...] + acc_ref[...].T


def _peer_dense(hnt, u_bf, vt_bf, s2, e2, r, e1, h1, tm, rows):
    d, t = hnt.shape
    n_exp = u_bf.shape[0]
    eb = rows * N_KEYS
    route = pl.BlockSpec((PEER_HEADS, N_KEYS, tm), lambda i, e: (0, 0, i))
    return pl.pallas_call(
        functools.partial(_dense_kernel, rows=rows),
        grid=(t // tm, n_exp // eb),
        in_specs=[
            pl.BlockSpec((d, tm), lambda i, e: (0, i)),
            pl.BlockSpec((eb, d), lambda i, e: (e, 0)),
            pl.BlockSpec((d, eb), lambda i, e: (0, e)),
            route, route, route, route,
            pl.BlockSpec((tm, d), lambda i, e: (i, 0)),
        ],
        out_specs=pl.BlockSpec((tm, d), lambda i, e: (i, 0)),
        out_shape=jax.ShapeDtypeStruct((t, d), F32),
        scratch_shapes=[pltpu.VMEM((d, tm), F32)],
        compiler_params=_cparams(("arbitrary", "arbitrary")),
        name="peer_dense",
    )(hnt, u_bf, vt_bf, s2, e2, r, e1, h1)


def _tiles(bsz, s):
    t = bsz * s
    return dict(
        proj=min(256, t),
        sb=min(256, s),
        chunk=min(256, s),
        merge=min(256, s),
        route=min(512, t),
        dense=min(512, t),
    )


def kernel(x, mem, g_mix, w_in, b_igate, b_fgate, conv_w, conv_b, ml_norm_g, mem_norm_g, w_mem_kv, q_norm_g, k_norm_g, w_sb_o, w_ml_o, w_mem_o, b_gate, w_out, g_ffn, w_peer_q, peer_k1, peer_k2, peer_u, peer_v):
    bsz, s, d = x.shape
    depth = g_mix.shape[0]
    tl = _tiles(bsz, s)
    sb_dim = SB_HEADS * SB_HD
    ml_dim = ML_HEADS * ML_HD
    mem_dim = MEM_HEADS * MEM_HD
    n1 = 3 * sb_dim + 3 * ml_dim
    n_if = 2 * ML_HEADS
    n2 = ml_dim + mem_dim + 3 * d

    h = x.reshape(bsz * s, d)
    for l in range(depth):
        w = w_in[l]
        w_if = jnp.pad(w[:, n1:n1 + n_if], ((0, 0), (0, LANES - n_if)))
        w_all = jnp.concatenate([w[:, :n1], w[:, n1 + n_if:], w_if], axis=1).astype(BF16)
        p1, p2, gates_if = _in_proj(h, g_mix[l][None], w_all, n1, n2, tl["proj"])

        y_sb = _stick_break(p1, bsz, s, tl["sb"])

        bias_if = jnp.pad(jnp.concatenate([b_igate[l], b_fgate[l]]), (0, LANES - n_if))[None]
        y_ml = _mlstm(p1, p2, gates_if, conv_w[l], conv_b[l][None], bias_if, ml_norm_g[l][None],
                      bsz, s, tl["chunk"])

        kh, vh = _mem_kv(mem, mem_norm_g[l][None], w_mem_kv[l].astype(BF16), k_norm_g[l][None])

        h1, hnt = _merge(y_sb, y_ml, p2, h, kh, vh, w_sb_o[l].astype(BF16), w_ml_o[l].astype(BF16),
                         w_mem_o[l].astype(BF16), w_out[l].astype(BF16), b_gate[l][None],
                         q_norm_g[l][None], g_ffn[l][None], s, tl["merge"])

        keys = jnp.stack([peer_k1[l], peer_k2[l]], axis=1).reshape(2 * PEER_HEADS, N_KEYS, -1).astype(BF16)
        s2, e2, r, e1 = _peer_route(hnt, w_peer_q[l].T.astype(BF16), keys, tl["route"])
        h = _peer_dense(hnt, peer_u[l].astype(BF16), peer_v[l].T.astype(BF16), s2, e2, r, e1, h1,
                        tl["dense"], 8)
    return h.reshape(bsz, s, d)
```

```python
import functools

import jax
import jax.numpy as jnp
from jax import lax
from jax.experimental import pallas as pl
from jax.experimental.pallas import tpu as pltpu

F32 = jnp.float32
BF16 = jnp.bfloat16

EPS = 1e-6
SB_HEADS, SB_HD = 8, 64
ML_HEADS, ML_HD = 4, 128
MEM_HEADS, MEM_HD = 4, 128
CONV_W = 4
N_KEYS = 128
PEER_HEADS = 8
PEER_TOPK = 16
LANES = 128
GATE_W = 256
VMEM_LIMIT = 56 << 20
NEG = -1e30
SB_DEAD = -104.0


def _cparams(sem):
    return pltpu.CompilerParams(dimension_semantics=sem, vmem_limit_bytes=VMEM_LIMIT)


def _rms(x, g):
    return x * lax.rsqrt(jnp.mean(x * x, axis=-1, keepdims=True) + EPS) * g


def _dot(a, b):
    return jnp.dot(a, b, preferred_element_type=F32)


def _dot_nt(a, b):
    return lax.dot_general(a, b, (((1,), (1,)), ((), ())), preferred_element_type=F32)


def _split_bf16(x):
    hi = x.astype(BF16)
    lo = (x - hi.astype(F32)).astype(BF16)
    return hi, lo


def _in_proj_kernel(x_ref, g_ref, w_ref, p1_ref, p2_ref, if_ref, *, n1, n2):
    xn = _rms(x_ref[...], g_ref[...]).astype(BF16)
    step = 1024
    for c in range(0, n1, step):
        p1_ref[:, c:c + step] = _dot(xn, w_ref[:, c:c + step]).astype(BF16)
    for c in range(0, n2, step):
        p2_ref[:, c:c + step] = _dot(xn, w_ref[:, n1 + c:n1 + c + step]).astype(BF16)
    if_ref[...] = _dot(xn, w_ref[:, n1 + n2:])


def _in_proj(x2, g, w_all, n1, n2, tm):
    t, d = x2.shape
    ncol = w_all.shape[1]
    return pl.pallas_call(
        functools.partial(_in_proj_kernel, n1=n1, n2=n2),
        grid=(t // tm,),
        in_specs=[
            pl.BlockSpec((tm, d), lambda i: (i, 0)),
            pl.BlockSpec((1, d), lambda i: (0, 0)),
            pl.BlockSpec((d, ncol), lambda i: (0, 0)),
        ],
        out_specs=[
            pl.BlockSpec((tm, n1), lambda i: (i, 0)),
            pl.BlockSpec((tm, n2), lambda i: (i, 0)),
            pl.BlockSpec((tm, LANES), lambda i: (i, 0)),
        ],
        out_shape=[
            jax.ShapeDtypeStruct((t, n1), BF16),
            jax.ShapeDtypeStruct((t, n2), BF16),
            jax.ShapeDtypeStruct((t, LANES), F32),
        ],
        compiler_params=_cparams(("arbitrary",)),
        name="in_proj",
    )(x2, g, w_all)


def _sb_kernel(q_ref, k_ref, v_ref, o_ref, *, tb, scale):
    i = pl.program_id(2)
    q = q_ref[...] * scale
    lane = lax.broadcasted_iota(jnp.int32, (tb, LANES), 1)
    first = lane < SB_HD
    zero = jnp.zeros_like(q)
    qs = (jnp.where(first, q, zero), jnp.where(first, zero, q))
    row = lax.broadcasted_iota(jnp.int32, (tb, tb), 0)
    col = lax.broadcasted_iota(jnp.int32, (tb, tb), 1)
    below = row > col
    ones_below = jnp.where(below, 1.0, 0.0).astype(BF16)

    def block(j, cs, acc, diag):
        k = k_ref[pl.ds(pl.multiple_of(j * tb, tb), tb), :]
        v = v_ref[pl.ds(pl.multiple_of(j * tb, tb), tb), :]
        outs, new_cs = [], []
        for qh, c in zip(qs, cs):
            z = _dot_nt(qh, k)
            lk = -(jnp.maximum(z, 0.0) + jnp.log(1.0 + jnp.exp(-jnp.abs(z))))
            if diag:
                lk = jnp.where(below, lk, 0.0)
            hi, lo = _split_bf16(lk)
            suffix = _dot(hi, ones_below) + _dot(lo, ones_below)
            a = jnp.exp(z + lk + suffix + c)
            if diag:
                a = jnp.where(below, a, 0.0)
            outs.append(_dot(a.astype(BF16), v))
            new_cs.append(c + jnp.sum(lk, axis=1, keepdims=True))
        acc = acc + jnp.where(first, outs[0], outs[1])
        return tuple(new_cs), acc

    c0 = jnp.zeros((tb, 1), F32)
    cs, acc = block(i, (c0, c0), jnp.zeros((tb, LANES), F32), True)

    def alive(carry):
        j, cs, _ = carry
        return jnp.logical_and(j >= 0, jnp.max(jnp.maximum(cs[0], cs[1])) > SB_DEAD)

    def body(carry):
        j, cs, acc = carry
        cs, acc = block(j, cs, acc, False)
        return j - 1, cs, acc

    _, _, acc = lax.while_loop(alive, body, (i - 1, cs, acc))
    o_ref[...] = acc.astype(o_ref.dtype)


def _stick_break(p1, bsz, s, tb):
    t = bsz * s
    nq = s // tb
    npair = SB_HEADS * SB_HD // LANES
    return pl.pallas_call(
        functools.partial(_sb_kernel, tb=tb, scale=SB_HD ** -0.5),
        grid=(bsz, npair, nq),
        in_specs=[
            pl.BlockSpec((tb, LANES), lambda b, p, i: (b * nq + i, p)),
            pl.BlockSpec((s, LANES), lambda b, p, i: (b, npair + p)),
            pl.BlockSpec((s, LANES), lambda b, p, i: (b, 2 * npair + p)),
        ],
        out_specs=pl.BlockSpec((tb, LANES), lambda b, p, i: (b * nq + i, p)),
        out_shape=jax.ShapeDtypeStruct((t, npair * LANES), BF16),
        compiler_params=_cparams(("arbitrary", "arbitrary", "arbitrary")),
        name="stick_break",
    )(p1, p1, p1)


def _mlstm_kernel(qr_ref, kr_ref, v_ref, qp_ref, kp_ref, if_ref, o_ref, cw_ref, cb_ref, bif_ref, ng_ref,
                  y_ref, qpad, kpad, c_st, n_st, m_st, *, L):
    ch = pl.program_id(1)
    dm = ML_HEADS * ML_HD

    @pl.when(ch == 0)
    def _():
        c_st[...] = jnp.zeros_like(c_st)
        n_st[...] = jnp.zeros_like(n_st)
        m_st[...] = jnp.zeros_like(m_st)

    keep_prev = jnp.where(ch == 0, 0.0, 1.0)
    convd = []
    for idx, (cur, prev, pad) in enumerate(((qr_ref, qp_ref, qpad), (kr_ref, kp_ref, kpad))):
        pad[0:8, :] = prev[...].astype(F32) * keep_prev
        pad[8:8 + L, :] = cur[...].astype(F32)
        y = cb_ref[:, idx * dm:(idx + 1) * dm]
        for tap in range(CONV_W):
            y = y + cw_ref[tap:tap + 1, idx * dm:(idx + 1) * dm] * pad[pl.ds(8 - (CONV_W - 1) + tap, L), :]
        convd.append(y * jax.nn.sigmoid(y))
    q_all = convd[0]
    k_all = convd[1] * (ML_HD ** -0.5)

    gates = if_ref[...] + bif_ref[...]
    lf = jnp.minimum(gates, 0.0) - jnp.log1p(jnp.exp(-jnp.abs(gates)))
    row = lax.broadcasted_iota(jnp.int32, (L, L), 0)
    col = lax.broadcasted_iota(jnp.int32, (L, L), 1)
    causal = col <= row
    tri = jnp.where(causal, 1.0, 0.0).astype(BF16)
    hi, lo = _split_bf16(lf)
    lo2 = (lf - hi.astype(F32) - lo.astype(F32)).astype(BF16)
    bcum = _dot(tri, hi) + _dot(tri, lo) + _dot(tri, lo2)
    lane = lax.broadcasted_iota(jnp.int32, (L, LANES), 1)
    packed = jnp.where(lane < ML_HEADS, gates, bcum)
    packed_t = packed.T

    for h in range(ML_HEADS):
        sl = slice(h * ML_HD, (h + 1) * ML_HD)
        q = q_all[:, sl]
        k = k_all[:, sl]
        v = v_ref[:, sl]
        qb = q.astype(BF16)
        ig_col = gates[:, h:h + 1]
        bc_col = bcum[:, ML_HEADS + h:ML_HEADS + h + 1]
        ig_row = packed_t[h:h + 1, :]
        bc_row = packed_t[ML_HEADS + h:ML_HEADS + h + 1, :]
        m_prev = m_st[h][0:1, 0:1]

        dmat = jnp.where(causal, bc_col - bc_row + ig_row, NEG)
        m_inter = bc_col + m_prev
        m_t = jnp.maximum(m_inter, jnp.max(dmat, axis=1, keepdims=True))
        w = _dot_nt(qb, k.astype(BF16)) * jnp.exp(dmat - m_t)
        inter = jnp.exp(m_inter - m_t)
        num = _dot(w.astype(BF16), v) + inter * _dot(qb, c_st[h].astype(BF16))
        den = jnp.sum(w, axis=1, keepdims=True) + inter * jnp.sum(q * n_st[h], axis=1, keepdims=True)
        hv = num / jnp.maximum(jnp.abs(den), jnp.exp(-m_t))

        b_last = bc_col[L - 1:L, :]
        g_col = b_last - bc_col + ig_col
        m_new = jnp.maximum(b_last + m_prev, jnp.max(g_col, axis=0, keepdims=True))
        ws_col = jnp.exp(g_col - m_new)
        decay = jnp.exp(b_last + m_prev - m_new)
        kw = k * ws_col
        c_st[h] = decay * c_st[h] + _dot(kw.T.astype(BF16), v)
        n_st[h] = decay * n_st[h] + jnp.sum(kw, axis=0, keepdims=True)
        m_st[h] = jnp.broadcast_to(m_new, (8, LANES))

        hn = _rms(hv, ng_ref[...])
        y_ref[:, sl] = (hn * jax.nn.sigmoid(o_ref[:, sl].astype(F32))).astype(y_ref.dtype)


def _mlstm(p1, p2, gates_if, conv_w, conv_b, bias_if, norm_g, bsz, s, L):
    t = bsz * s
    nc = s // L
    dm = ML_HEADS * ML_HD
    lb = L // 8

    def prev_map(b, c):
        return (jnp.maximum((b * nc + c) * lb - 1, 0), 0)

    return pl.pallas_call(
        functools.partial(_mlstm_kernel, L=L),
        grid=(bsz, nc),
        in_specs=[
            pl.BlockSpec((L, dm), lambda b, c: (b * nc + c, 3)),
            pl.BlockSpec((L, dm), lambda b, c: (b * nc + c, 4)),
            pl.BlockSpec((L, dm), lambda b, c: (b * nc + c, 5)),
            pl.BlockSpec((8, dm), lambda b, c: (prev_map(b, c)[0], 3)),
            pl.BlockSpec((8, dm), lambda b, c: (prev_map(b, c)[0], 4)),
            pl.BlockSpec((L, LANES), lambda b, c: (b * nc + c, 0)),
            pl.BlockSpec((L, dm), lambda b, c: (b * nc + c, 0)),
            pl.BlockSpec((CONV_W, 2 * dm), lambda b, c: (0, 0)),
            pl.BlockSpec((1, 2 * dm), lambda b, c: (0, 0)),
            pl.BlockSpec((1, LANES), lambda b, c: (0, 0)),
            pl.BlockSpec((1, ML_HD), lambda b, c: (0, 0)),
        ],
        out_specs=pl.BlockSpec((L, dm), lambda b, c: (b * nc + c, 0)),
        out_shape=jax.ShapeDtypeStruct((t, dm), BF16),
        scratch_shapes=[
            pltpu.VMEM((L + 8, dm), F32),
            pltpu.VMEM((L + 8, dm), F32),
            pltpu.VMEM((ML_HEADS, ML_HD, ML_HD), F32),
            pltpu.VMEM((ML_HEADS, 1, ML_HD), F32),
            pltpu.VMEM((ML_HEADS, 8, LANES), F32),
        ],
        compiler_params=_cparams(("arbitrary", "arbitrary")),
        name="mlstm",
    )(p1, p1, p1, p1, p1, gates_if, p2, conv_w, conv_b, bias_if, norm_g)


def _mem_kv_kernel(mem_ref, g_ref, w_ref, kg_ref, k_ref, v_ref):
    memn = _rms(mem_ref[0], g_ref[...]).astype(BF16)
    kv = _dot(memn, w_ref[...])
    dm = MEM_HEADS * MEM_HD
    for h in range(MEM_HEADS):
        sl = slice(h * MEM_HD, (h + 1) * MEM_HD)
        k_ref[0, :, sl] = _rms(kv[:, sl], kg_ref[...]).astype(BF16)
    v_ref[0] = kv[:, dm:].astype(BF16)


def _mem_kv(mem, g, w_kv, k_g):
    bsz, ml, d = mem.shape
    dm = MEM_HEADS * MEM_HD
    return pl.pallas_call(
        _mem_kv_kernel,
        grid=(bsz,),
        in_specs=[
            pl.BlockSpec((1, ml, d), lambda b: (b, 0, 0)),
            pl.BlockSpec((1, d), lambda b: (0, 0)),
            pl.BlockSpec((d, 2 * dm), lambda b: (0, 0)),
            pl.BlockSpec((1, MEM_HD), lambda b: (0, 0)),
        ],
        out_specs=[
            pl.BlockSpec((1, ml, dm), lambda b: (b, 0, 0)),
            pl.BlockSpec((1, ml, dm), lambda b: (b, 0, 0)),
        ],
        out_shape=[jax.ShapeDtypeStruct((bsz, ml, dm), BF16)] * 2,
        compiler_params=_cparams(("arbitrary",)),
        name="mem_kv",
    )(mem, g, w_kv, k_g)


def _merge_kernel(ysb_ref, yml_ref, mq_ref, g0_ref, g1_ref, g2_ref, x_ref, kh_ref, vh_ref,
                  wsb_ref, wml_ref, wmem_ref, wout_ref, bg_ref, qg_ref, gf_ref, h_ref, hnt_ref):
    d = x_ref.shape[1]
    heads = []
    for h in range(MEM_HEADS):
        sl = slice(h * MEM_HD, (h + 1) * MEM_HD)
        qh = _rms(mq_ref[:, sl].astype(F32), qg_ref[...]).astype(BF16)
        sc = _dot_nt(qh, kh_ref[0, :, sl]) * (MEM_HD ** -0.5)
        p = jnp.exp(sc - jnp.max(sc, axis=1, keepdims=True))
        p = p / jnp.sum(p, axis=1, keepdims=True)
        heads.append(_dot(p.astype(BF16), vh_ref[0, :, sl]))
    ymem = jnp.concatenate(heads, axis=1).astype(BF16)

    def gate(ref, idx):
        return jax.nn.sigmoid(ref[...].astype(F32) + bg_ref[:, idx * d:(idx + 1) * d])

    merged = (gate(g0_ref, 0) * _dot(ysb_ref[...], wsb_ref[...])
              + gate(g1_ref, 1) * _dot(yml_ref[...], wml_ref[...])
              + gate(g2_ref, 2) * _dot(ymem, wmem_ref[...]))
    h1 = x_ref[...] + _dot(merged.astype(BF16), wout_ref[...])
    h_ref[...] = h1
    hnt_ref[...] = _rms(h1, gf_ref[...]).T.astype(BF16)


def _merge(ysb, yml, p2, x2, kh, vh, wsb, wml, wmem, wout, b_gate, q_g, g_ffn, s, tm):
    t, d = x2.shape
    dm = MEM_HEADS * MEM_HD
    ml = kh.shape[1]
    per_b = s // tm
    const = lambda i: (0, 0)
    return pl.pallas_call(
        _merge_kernel,
        grid=(t // tm,),
        in_specs=[
            pl.BlockSpec((tm, dm), lambda i: (i, 0)),
            pl.BlockSpec((tm, dm), lambda i: (i, 0)),
            pl.BlockSpec((tm, dm), lambda i: (i, 1)),
            pl.BlockSpec((tm, d), lambda i: (i, 1)),
            pl.BlockSpec((tm, d), lambda i: (i, 2)),
            pl.BlockSpec((tm, d), lambda i: (i, 3)),
            pl.BlockSpec((tm, d), lambda i: (i, 0)),
            pl.BlockSpec((1, ml, dm), lambda i: (i // per_b, 0, 0)),
            pl.BlockSpec((1, ml, dm), lambda i: (i // per_b, 0, 0)),
            pl.BlockSpec((dm, d), const),
            pl.BlockSpec((dm, d), const),
            pl.BlockSpec((dm, d), const),
            pl.BlockSpec((d, d), const),
            pl.BlockSpec((1, 3 * d), const),
            pl.BlockSpec((1, MEM_HD), const),
            pl.BlockSpec((1, d), const),
        ],
        out_specs=[
            pl.BlockSpec((tm, d), lambda i: (i, 0)),
            pl.BlockSpec((d, tm), lambda i: (0, i)),
        ],
        out_shape=[jax.ShapeDtypeStruct((t, d), F32), jax.ShapeDtypeStruct((d, t), BF16)],
        compiler_params=_cparams(("arbitrary",)),
        name="merge",
    )(ysb, yml, p2, p2, p2, p2, x2, kh, vh, wsb, wml, wmem, wout, b_gate, q_g, g_ffn)


_TAKEN = 2.0 ** 100


def _top16(x):
    vals = []
    for k in range(PEER_TOPK):
        m = jnp.max(x, axis=0, keepdims=True)
        vals.append(m)
        x = jnp.where(x == m, -_TAKEN * (1.0 + k / 16.0), x)
    rank = jnp.where(x < -0.5 * _TAKEN, x * (-16.0 / _TAKEN) - 16.0, float(PEER_TOPK))
    return vals, rank


_PAIRS = [(a, b) for a in range(PEER_TOPK) for b in range(PEER_TOPK) if (a + 1) * (b + 1) <= PEER_TOPK]
_N_CAND = -(-len(_PAIRS) // 8) * 8


def _route_kernel(hnt_ref, wq_ref, keys_ref, rank2_ref, e2_ref, n1_ref, e1_ref, qt_ref, cand_ref):
    tm = hnt_ref.shape[1]
    qt_ref[...] = _dot(wq_ref[...], hnt_ref[...]).astype(BF16)
    cand_ref[...] = jnp.full(cand_ref.shape, NEG, F32)

    def head(h, carry):
        r0 = pl.multiple_of(h * (2 * N_KEYS), 2 * N_KEYS)
        s1_all = _dot(keys_ref[2 * h], qt_ref[pl.ds(r0, N_KEYS), :])
        s2_all = _dot(keys_ref[2 * h + 1], qt_ref[pl.ds(r0 + N_KEYS, N_KEYS), :])
        for c in range(tm // LANES):
            ls = slice(c * LANES, (c + 1) * LANES)
            s1, s2 = s1_all[:, ls], s2_all[:, ls]
            v1, rank1 = _top16(s1)
            v2, rank2 = _top16(s2)
            sums = {}
            for n, (a, b) in enumerate(_PAIRS):
                sums[a, b] = v1[a] + v2[b]
                cand_ref[n:n + 1, ls] = sums[a, b]
            top, _ = _top16(cand_ref[:, ls])
            thr = top[PEER_TOPK - 1]
            zsum = jnp.ones_like(thr)
            for r in range(1, PEER_TOPK):
                zsum = zsum + jnp.exp(top[r] - top[0])
            n1 = jnp.zeros_like(s1)
            for a in range(PEER_TOPK):
                cnt = jnp.zeros_like(thr)
                for b in range(PEER_TOPK):
                    if (a, b) in sums:
                        cnt = cnt + jnp.where(sums[a, b] >= thr, 1.0, 0.0)
                n1 = jnp.where(rank1 == float(a), cnt, n1)
            n1_ref[h, :, ls] = n1
            e1_ref[h, :, ls] = jnp.exp(s1 - v1[0]) * (0.5 / zsum)
            rank2_ref[h, :, ls] = rank2.astype(BF16)
            e2_ref[h, :, ls] = jnp.exp(s2 - v2[0]).astype(BF16)
        return carry

    lax.fori_loop(0, PEER_HEADS, head, 0)


def _peer_route(hnt, wq_t, keys, tm):
    d, t = hnt.shape
    nq = wq_t.shape[0]
    blk = pl.BlockSpec((PEER_HEADS, N_KEYS, tm), lambda i: (0, 0, i))
    return pl.pallas_call(
        _route_kernel,
        grid=(t // tm,),
        in_specs=[
            pl.BlockSpec((d, tm), lambda i: (0, i)),
            pl.BlockSpec((nq, d), lambda i: (0, 0)),
            pl.BlockSpec((2 * PEER_HEADS, N_KEYS, N_KEYS), lambda i: (0, 0, 0)),
        ],
        out_specs=[blk] * 4,
        out_shape=[jax.ShapeDtypeStruct((PEER_HEADS, N_KEYS, t), BF16)] * 2
        + [jax.ShapeDtypeStruct((PEER_HEADS, N_KEYS, t), F32)] * 2,
        scratch_shapes=[pltpu.VMEM((nq, tm), BF16), pltpu.VMEM((_N_CAND, tm), F32)],
        compiler_params=_cparams(("arbitrary",)),
        name="peer_route",
    )(hnt, wq_t, keys)


def _dense_kernel(hnt_ref, u_ref, vprev_ref, vlast_ref, rank2_ref, e2_ref, n1_ref, e1_ref, h_ref, o_ref,
                  acc_ref, act_ref, w_ref, *, rows):
    e = pl.program_id(1)
    tm = hnt_ref.shape[1]
    cur = lax.rem(e, 2)

    @pl.when(e == 0)
    def _():
        acc_ref[...] = jnp.zeros_like(acc_ref)
        w_ref[1] = jnp.zeros(w_ref.shape[1:], w_ref.dtype)

    acc_ref[...] += _dot(vprev_ref[...], w_ref[1 - cur])
    act_ref[...] = _dot(u_ref[...], hnt_ref[...])
    zero = jnp.zeros((), BF16)
    for ii in range(rows):
        i = e * rows + ii
        rs = slice(ii * N_KEYS, (ii + 1) * N_KEYS)
        n_rows = [n1_ref[h, pl.ds(i, 1), :] for h in range(PEER_HEADS)]
        e_rows = [e1_ref[h, pl.ds(i, 1), :] for h in range(PEER_HEADS)]
        for c in range(tm // GATE_W):
            ls = slice(c * GATE_W, (c + 1) * GATE_W)
            gate = None
            for h in range(PEER_HEADS):
                nb = jnp.broadcast_to(n_rows[h][:, ls], (N_KEYS, GATE_W)).astype(BF16)
                eb = jnp.broadcast_to(e_rows[h][:, ls], (N_KEYS, GATE_W)).astype(BF16)
                term = jnp.where(rank2_ref[h, :, ls] < nb, e2_ref[h, :, ls] * eb, zero)
                gate = term if gate is None else gate + term
            a = act_ref[rs, ls]
            w_ref[cur, rs, ls] = (a * (1.0 + lax.erf(a * (2.0 ** -0.5)))).astype(BF16) * gate

    @pl.when(e == pl.num_programs(1) - 1)
    def _():
        acc = acc_ref[...] + _dot(vlast_ref[...], w_ref[cur])
        o_ref[...] = h_ref[...] + acc.T


def _peer_dense(hnt, u_bf, vt_bf, rank2, e2, n1, e1, h1, tm, rows):
    d, t = hnt.shape
    n_exp = u_bf.shape[0]
    eb = rows * N_KEYS
    ne = n_exp // eb
    by_key = by_row = pl.BlockSpec((PEER_HEADS, N_KEYS, tm), lambda i, e: (0, 0, i))
    return pl.pallas_call(
        functools.partial(_dense_kernel, rows=rows),
        grid=(t // tm, ne),
        in_specs=[
            pl.BlockSpec((d, tm), lambda i, e: (0, i)),
            pl.BlockSpec((eb, d), lambda i, e: (e, 0)),
            pl.BlockSpec((d, eb), lambda i, e: (0, jnp.maximum(e - 1, 0))),
            pl.BlockSpec((d, eb), lambda i, e: (0, ne - 1)),
            by_key, by_key, by_row, by_row,
            pl.BlockSpec((tm, d), lambda i, e: (i, 0)),
        ],
        out_specs=pl.BlockSpec((tm, d), lambda i, e: (i, 0)),
        out_shape=jax.ShapeDtypeStruct((t, d), F32),
        scratch_shapes=[
            pltpu.VMEM((d, tm), F32),
            pltpu.VMEM((eb, tm), F32),
            pltpu.VMEM((2, eb, tm), BF16),
        ],
        compiler_params=_cparams(("arbitrary", "arbitrary")),
        name="peer_dense",
    )(hnt, u_bf, vt_bf, vt_bf, rank2, e2, n1, e1, h1)


def _tiles(bsz, s):
    t = bsz * s
    return dict(
        proj=min(256, t),
        sb=min(256, s),
        chunk=min(256, s),
        merge=min(256, s),
        route=min(512, t),
        dense=min(512, t),
    )


def kernel(x, mem, g_mix, w_in, b_igate, b_fgate, conv_w, conv_b, ml_norm_g, mem_norm_g, w_mem_kv, q_norm_g, k_norm_g, w_sb_o, w_ml_o, w_mem_o, b_gate, w_out, g_ffn, w_peer_q, peer_k1, peer_k2, peer_u, peer_v):
    bsz, s, d = x.shape
    depth = g_mix.shape[0]
    tl = _tiles(bsz, s)
    sb_dim = SB_HEADS * SB_HD
    ml_dim = ML_HEADS * ML_HD
    mem_dim = MEM_HEADS * MEM_HD
    n1 = 3 * sb_dim + 3 * ml_dim
    n_if = 2 * ML_HEADS
    n2 = ml_dim + mem_dim + 3 * d

    h = x.reshape(bsz * s, d)
    for l in range(depth):
        w = w_in[l]
        w_if = jnp.pad(w[:, n1:n1 + n_if], ((0, 0), (0, LANES - n_if)))
        w_all = jnp.concatenate([w[:, :n1], w[:, n1 + n_if:], w_if], axis=1).astype(BF16)
        p1, p2, gates_if = _in_proj(h, g_mix[l][None], w_all, n1, n2, tl["proj"])

        y_sb = _stick_break(p1, bsz, s, tl["sb"])

        bias_if = jnp.pad(jnp.concatenate([b_igate[l], b_fgate[l]]), (0, LANES - n_if))[None]
        y_ml = _mlstm(p1, p2, gates_if, conv_w[l], conv_b[l][None], bias_if, ml_norm_g[l][None],
                      bsz, s, tl["chunk"])

        kh, vh = _mem_kv(mem, mem_norm_g[l][None], w_mem_kv[l].astype(BF16), k_norm_g[l][None])

        h1, hnt = _merge(y_sb, y_ml, p2, h, kh, vh, w_sb_o[l].astype(BF16), w_ml_o[l].astype(BF16),
                         w_mem_o[l].astype(BF16), w_out[l].astype(BF16), b_gate[l][None],
                         q_norm_g[l][None], g_ffn[l][None], s, tl["merge"])

        keys = jnp.stack([peer_k1[l], peer_k2[l]], axis=1).reshape(2 * PEER_HEADS, N_KEYS, -1).astype(BF16)
        rank2, e2, n1, e1 = _peer_route(hnt, w_peer_q[l].T.astype(BF16), keys, tl["route"])
        h = _peer_dense(hnt, peer_u[l].astype(BF16), peer_v[l].T.astype(BF16), rank2, e2, n1, e1, h1,
                        tl["dense"], 8)
    return h.reshape(bsz, s, d)
```

```python
import functools

import jax
import jax.numpy as jnp
from jax import lax
from jax.experimental import pallas as pl
from jax.experimental.pallas import tpu as pltpu

F32 = jnp.float32
BF16 = jnp.bfloat16

EPS = 1e-6
SB_HEADS, SB_HD = 8, 64
SB_PAIRS = 2
ML_HEADS, ML_HD = 4, 128
MEM_HEADS, MEM_HD = 4, 128
MERGE_SUB = 2
CONV_W = 4
N_KEYS = 128
PEER_HEADS = 8
PEER_TOPK = 16
LANES = 128
DENSE_ROWS = 8
GATE_W = 256
VMEM_LIMIT = 56 << 20
NEG = -1e30
SB_DEAD = -104.0


def _cparams(sem):
    return pltpu.CompilerParams(dimension_semantics=sem, vmem_limit_bytes=VMEM_LIMIT)


def _rms(x, g):
    return x * lax.rsqrt(jnp.mean(x * x, axis=-1, keepdims=True) + EPS) * g


def _dot(a, b):
    return jnp.dot(a, b, preferred_element_type=F32)


def _dot_nt(a, b):
    return lax.dot_general(a, b, (((1,), (1,)), ((), ())), preferred_element_type=F32)


def _split_bf16(x):
    hi = x.astype(BF16)
    lo = (x - hi.astype(F32)).astype(BF16)
    return hi, lo


def _in_proj_kernel(x_ref, g_ref, w_ref, p1_ref, p2_ref, if_ref, *, n1, n2):
    xn = _rms(x_ref[...], g_ref[...]).astype(BF16)
    step = 1024
    for c in range(0, n1, step):
        p1_ref[:, c:c + step] = _dot(xn, w_ref[:, c:c + step]).astype(BF16)
    for c in range(0, n2, step):
        p2_ref[:, c:c + step] = _dot(xn, w_ref[:, n1 + c:n1 + c + step]).astype(BF16)
    if_ref[...] = _dot(xn, w_ref[:, n1 + n2:])


def _in_proj(x2, g, w_all, n1, n2, tm):
    t, d = x2.shape
    ncol = w_all.shape[1]
    return pl.pallas_call(
        functools.partial(_in_proj_kernel, n1=n1, n2=n2),
        grid=(t // tm,),
        in_specs=[
            pl.BlockSpec((tm, d), lambda i: (i, 0)),
            pl.BlockSpec((1, d), lambda i: (0, 0)),
            pl.BlockSpec((d, ncol), lambda i: (0, 0)),
        ],
        out_specs=[
            pl.BlockSpec((tm, n1), lambda i: (i, 0)),
            pl.BlockSpec((tm, n2), lambda i: (i, 0)),
            pl.BlockSpec((tm, LANES), lambda i: (i, 0)),
        ],
        out_shape=[
            jax.ShapeDtypeStruct((t, n1), BF16),
            jax.ShapeDtypeStruct((t, n2), BF16),
            jax.ShapeDtypeStruct((t, LANES), F32),
        ],
        compiler_params=_cparams(("arbitrary",)),
        name="in_proj",
    )(x2, g, w_all)


def _sb_kernel(*refs, tb, scale, pairs):
    q_refs, k_refs, v_refs, o_ref = refs[:pairs], refs[pairs:2 * pairs], refs[2 * pairs:3 * pairs], refs[3 * pairs]
    i = pl.program_id(2)
    heads = range(2 * pairs)
    lane = lax.broadcasted_iota(jnp.int32, (tb, LANES), 1)
    first = lane < SB_HD
    qs = []
    for g in range(pairs):
        q = q_refs[g][...] * scale
        zero = jnp.zeros_like(q)
        qs += [jnp.where(first, q, zero), jnp.where(first, zero, q)]
    row = lax.broadcasted_iota(jnp.int32, (tb, tb), 0)
    col = lax.broadcasted_iota(jnp.int32, (tb, tb), 1)
    below = row > col
    ones_below = jnp.where(below, 1.0, 0.0).astype(BF16)

    def block(j, cs, accs, diag):
        rows = pl.ds(pl.multiple_of(j * tb, tb), tb)
        ks = [k_refs[g][rows, :] for g in range(pairs)]
        vs = [v_refs[g][rows, :] for g in range(pairs)]
        zs = [_dot_nt(qs[n], ks[n // 2]) for n in heads]
        lks = []
        for n in heads:
            lk = -(jnp.maximum(zs[n], 0.0) + jnp.log(1.0 + jnp.exp(-jnp.abs(zs[n]))))
            lks.append(jnp.where(below, lk, 0.0) if diag else lk)
        sufs = []
        for n in heads:
            hi, lo = _split_bf16(lks[n])
            sufs.append(_dot(hi, ones_below) + _dot(lo, ones_below))
        outs = []
        for n in heads:
            a = jnp.exp(zs[n] + lks[n] + sufs[n] + cs[n])
            if diag:
                a = jnp.where(below, a, 0.0)
            outs.append(_dot(a.astype(BF16), vs[n // 2]))
        new_cs = tuple(cs[n] + jnp.sum(lks[n], axis=1, keepdims=True) for n in heads)
        new_accs = tuple(accs[g] + jnp.where(first, outs[2 * g], outs[2 * g + 1]) for g in range(pairs))
        return new_cs, new_accs

    c0 = jnp.zeros((tb, 1), F32)
    a0 = jnp.zeros((tb, LANES), F32)
    cs, accs = block(i, (c0,) * (2 * pairs), (a0,) * pairs, True)

    def alive(carry):
        j, cs, _ = carry
        return jnp.logical_and(j >= 0, jnp.max(functools.reduce(jnp.maximum, cs)) > SB_DEAD)

    def body(carry):
        j, cs, accs = carry
        cs, accs = block(j, cs, accs, False)
        return j - 1, cs, accs

    _, _, accs = lax.while_loop(alive, body, (i - 1, cs, accs))
    for g in range(pairs):
        o_ref[:, g * LANES:(g + 1) * LANES] = accs[g].astype(o_ref.dtype)


def _stick_break(p1, bsz, s, tb, pairs):
    t = bsz * s
    nq = s // tb
    npair = SB_HEADS * SB_HD // LANES
    ng = npair // pairs

    def query_col(g):
        return pl.BlockSpec((tb, LANES), lambda b, p, i: (b * nq + i, p * pairs + g))

    def seq_col(first_col, g):
        return pl.BlockSpec((s, LANES), lambda b, p, i: (b, first_col + p * pairs + g))

    groups = range(pairs)
    return pl.pallas_call(
        functools.partial(_sb_kernel, tb=tb, scale=SB_HD ** -0.5, pairs=pairs),
        grid=(bsz, ng, nq),
        in_specs=([query_col(g) for g in groups] + [seq_col(npair, g) for g in groups]
                  + [seq_col(2 * npair, g) for g in groups]),
        out_specs=pl.BlockSpec((tb, pairs * LANES), lambda b, p, i: (b * nq + i, p)),
        out_shape=jax.ShapeDtypeStruct((t, npair * LANES), BF16),
        compiler_params=_cparams(("arbitrary", "arbitrary", "arbitrary")),
        name="stick_break",
    )(*([p1] * (3 * pairs)))


def _mlstm_kernel(qr_ref, kr_ref, v_ref, qp_ref, kp_ref, if_ref, o_ref, cw_ref, cb_ref, bif_ref, ng_ref,
                  y_ref, qpad, kpad, c_st, n_st, m_st, *, L):
    ch = pl.program_id(1)
    dm = ML_HEADS * ML_HD

    @pl.when(ch == 0)
    def _():
        c_st[...] = jnp.zeros_like(c_st)
        n_st[...] = jnp.zeros_like(n_st)
        m_st[...] = jnp.zeros_like(m_st)

    keep_prev = jnp.where(ch == 0, 0.0, 1.0)
    convd = []
    for idx, (cur, prev, pad) in enumerate(((qr_ref, qp_ref, qpad), (kr_ref, kp_ref, kpad))):
        pad[0:8, :] = prev[...].astype(F32) * keep_prev
        pad[8:8 + L, :] = cur[...].astype(F32)
        y = cb_ref[:, idx * dm:(idx + 1) * dm]
        for tap in range(CONV_W):
            y = y + cw_ref[tap:tap + 1, idx * dm:(idx + 1) * dm] * pad[pl.ds(8 - (CONV_W - 1) + tap, L), :]
        convd.append(y * jax.nn.sigmoid(y))
    q_all = convd[0]
    k_all = convd[1] * (ML_HD ** -0.5)

    gates = if_ref[...] + bif_ref[...]
    lf = jnp.minimum(gates, 0.0) - jnp.log1p(jnp.exp(-jnp.abs(gates)))
    row = lax.broadcasted_iota(jnp.int32, (L, L), 0)
    col = lax.broadcasted_iota(jnp.int32, (L, L), 1)
    causal = col <= row
    tri = jnp.where(causal, 1.0, 0.0).astype(BF16)
    hi, lo = _split_bf16(lf)
    lo2 = (lf - hi.astype(F32) - lo.astype(F32)).astype(BF16)
    bcum = _dot(tri, hi) + _dot(tri, lo) + _dot(tri, lo2)
    lane = lax.broadcasted_iota(jnp.int32, (L, LANES), 1)
    packed = jnp.where(lane < ML_HEADS, gates, bcum)
    packed_t = packed.T

    hds = range(ML_HEADS)
    sl = [slice(h * ML_HD, (h + 1) * ML_HD) for h in hds]
    q = [q_all[:, sl[h]] for h in hds]
    k = [k_all[:, sl[h]] for h in hds]
    v = [v_ref[:, sl[h]] for h in hds]
    qb = [q[h].astype(BF16) for h in hds]
    ig_col = [gates[:, h:h + 1] for h in hds]
    bc_col = [bcum[:, ML_HEADS + h:ML_HEADS + h + 1] for h in hds]
    ig_row = [packed_t[h:h + 1, :] for h in hds]
    bc_row = [packed_t[ML_HEADS + h:ML_HEADS + h + 1, :] for h in hds]
    m_prev = [m_st[h][0:1, 0:1] for h in hds]

    qk = [_dot_nt(qb[h], k[h].astype(BF16)) for h in hds]
    qc = [_dot(qb[h], c_st[h].astype(BF16)) for h in hds]
    dmat = [jnp.where(causal, bc_col[h] - bc_row[h] + ig_row[h], NEG) for h in hds]
    m_inter = [bc_col[h] + m_prev[h] for h in hds]
    m_t = [jnp.maximum(m_inter[h], jnp.max(dmat[h], axis=1, keepdims=True)) for h in hds]
    w = [qk[h] * jnp.exp(dmat[h] - m_t[h]) for h in hds]
    inter = [jnp.exp(m_inter[h] - m_t[h]) for h in hds]
    wv = [_dot(w[h].astype(BF16), v[h]) for h in hds]

    b_last = [bc_col[h][L - 1:L, :] for h in hds]
    g_col = [b_last[h] - bc_col[h] + ig_col[h] for h in hds]
    m_new = [jnp.maximum(b_last[h] + m_prev[h], jnp.max(g_col[h], axis=0, keepdims=True)) for h in hds]
    kw = [k[h] * jnp.exp(g_col[h] - m_new[h]) for h in hds]
    decay = [jnp.exp(b_last[h] + m_prev[h] - m_new[h]) for h in hds]
    kwv = [_dot(kw[h].T.astype(BF16), v[h]) for h in hds]

    den = [jnp.sum(w[h], axis=1, keepdims=True) + inter[h] * jnp.sum(q[h] * n_st[h], axis=1, keepdims=True)
           for h in hds]
    hv = [(wv[h] + inter[h] * qc[h]) / jnp.maximum(jnp.abs(den[h]), jnp.exp(-m_t[h])) for h in hds]
    for h in hds:
        c_st[h] = decay[h] * c_st[h] + kwv[h]
        n_st[h] = decay[h] * n_st[h] + jnp.sum(kw[h], axis=0, keepdims=True)
        m_st[h] = jnp.broadcast_to(m_new[h], (8, LANES))
    for h in hds:
        hn = _rms(hv[h], ng_ref[...])
        y_ref[:, sl[h]] = (hn * jax.nn.sigmoid(o_ref[:, sl[h]].astype(F32))).astype(y_ref.dtype)


def _mlstm(p1, p2, gates_if, conv_w, conv_b, bias_if, norm_g, bsz, s, L):
    t = bsz * s
    nc = s // L
    dm = ML_HEADS * ML_HD
    lb = L // 8

    def prev_map(b, c):
        return (jnp.maximum((b * nc + c) * lb - 1, 0), 0)

    return pl.pallas_call(
        functools.partial(_mlstm_kernel, L=L),
        grid=(bsz, nc),
        in_specs=[
            pl.BlockSpec((L, dm), lambda b, c: (b * nc + c, 3)),
            pl.BlockSpec((L, dm), lambda b, c: (b * nc + c, 4)),
            pl.BlockSpec((L, dm), lambda b, c: (b * nc + c, 5)),
            pl.BlockSpec((8, dm), lambda b, c: (prev_map(b, c)[0], 3)),
            pl.BlockSpec((8, dm), lambda b, c: (prev_map(b, c)[0], 4)),
            pl.BlockSpec((L, LANES), lambda b, c: (b * nc + c, 0)),
            pl.BlockSpec((L, dm), lambda b, c: (b * nc + c, 0)),
            pl.BlockSpec((CONV_W, 2 * dm), lambda b, c: (0, 0)),
            pl.BlockSpec((1, 2 * dm), lambda b, c: (0, 0)),
            pl.BlockSpec((1, LANES), lambda b, c: (0, 0)),
            pl.BlockSpec((1, ML_HD), lambda b, c: (0, 0)),
        ],
        out_specs=pl.BlockSpec((L, dm), lambda b, c: (b * nc + c, 0)),
        out_shape=jax.ShapeDtypeStruct((t, dm), BF16),
        scratch_shapes=[
            pltpu.VMEM((L + 8, dm), F32),
            pltpu.VMEM((L + 8, dm), F32),
            pltpu.VMEM((ML_HEADS, ML_HD, ML_HD), F32),
            pltpu.VMEM((ML_HEADS, 1, ML_HD), F32),
            pltpu.VMEM((ML_HEADS, 8, LANES), F32),
        ],
        compiler_params=_cparams(("arbitrary", "arbitrary")),
        name="mlstm",
    )(p1, p1, p1, p1, p1, gates_if, p2, conv_w, conv_b, bias_if, norm_g)


def _mem_kv_kernel(mem_ref, g_ref, w_ref, kg_ref, k_ref, v_ref):
    memn = _rms(mem_ref[0], g_ref[...]).astype(BF16)
    kv = _dot(memn, w_ref[...])
    dm = MEM_HEADS * MEM_HD
    for h in range(MEM_HEADS):
        sl = slice(h * MEM_HD, (h + 1) * MEM_HD)
        k_ref[0, :, sl] = _rms(kv[:, sl], kg_ref[...]).astype(BF16)
    v_ref[0] = kv[:, dm:].astype(BF16)


def _mem_kv(mem, g, w_kv, k_g):
    bsz, ml, d = mem.shape
    dm = MEM_HEADS * MEM_HD
    return pl.pallas_call(
        _mem_kv_kernel,
        grid=(bsz,),
        in_specs=[
            pl.BlockSpec((1, ml, d), lambda b: (b, 0, 0)),
            pl.BlockSpec((1, d), lambda b: (0, 0)),
            pl.BlockSpec((d, 2 * dm), lambda b: (0, 0)),
            pl.BlockSpec((1, MEM_HD), lambda b: (0, 0)),
        ],
        out_specs=[
            pl.BlockSpec((1, ml, dm), lambda b: (b, 0, 0)),
            pl.BlockSpec((1, ml, dm), lambda b: (b, 0, 0)),
        ],
        out_shape=[jax.ShapeDtypeStruct((bsz, ml, dm), BF16)] * 2,
        compiler_params=_cparams(("arbitrary",)),
        name="mem_kv",
    )(mem, g, w_kv, k_g)


def _merge_kernel(ysb_ref, yml_ref, mq_ref, g0_ref, g1_ref, g2_ref, x_ref, kh_ref, vh_ref,
                  wsb_ref, wml_ref, wmem_ref, wout_ref, bg_ref, qg_ref, gf_ref, h_ref, hnt_ref):
    tm, d = x_ref.shape
    sub = tm // MERGE_SUB
    rws = [slice(r * sub, (r + 1) * sub) for r in range(MERGE_SUB)]
    hds = [slice(h * MEM_HD, (h + 1) * MEM_HD) for h in range(MEM_HEADS)]
    qh = [[_rms(mq_ref[rs, hs].astype(F32), qg_ref[...]).astype(BF16) for hs in hds] for rs in rws]
    sc = [[_dot_nt(q, kh_ref[0, :, hs]) * (MEM_HD ** -0.5) for q, hs in zip(qr, hds)] for qr in qh]
    p = [[jnp.exp(s - jnp.max(s, axis=1, keepdims=True)) for s in sr] for sr in sc]
    p = [[(x / jnp.sum(x, axis=1, keepdims=True)).astype(BF16) for x in pr] for pr in p]
    att = [[_dot(x, vh_ref[0, :, hs]) for x, hs in zip(pr, hds)] for pr in p]
    ymem = [jnp.concatenate(ar, axis=1).astype(BF16) for ar in att]
    dsb = [_dot(ysb_ref[rs, :], wsb_ref[...]) for rs in rws]
    dml = [_dot(yml_ref[rs, :], wml_ref[...]) for rs in rws]
    dmem = [_dot(y, wmem_ref[...]) for y in ymem]

    def gate(ref, rs, idx):
        return jax.nn.sigmoid(ref[rs, :].astype(F32) + bg_ref[:, idx * d:(idx + 1) * d])

    merged = [(gate(g0_ref, rs, 0) * a + gate(g1_ref, rs, 1) * b + gate(g2_ref, rs, 2) * c).astype(BF16)
              for rs, a, b, c in zip(rws, dsb, dml, dmem)]
    h1 = [x_ref[rs, :] + _dot(m, wout_ref[...]) for rs, m in zip(rws, merged)]
    for rs, hh in zip(rws, h1):
        h_ref[rs, :] = hh
        hnt_ref[:, rs] = _rms(hh, gf_ref[...]).T.astype(BF16)


def _merge(ysb, yml, p2, x2, kh, vh, wsb, wml, wmem, wout, b_gate, q_g, g_ffn, s, tm):
    t, d = x2.shape
    dm = MEM_HEADS * MEM_HD
    ml = kh.shape[1]
    per_b = s // tm
    const = lambda i: (0, 0)
    return pl.pallas_call(
        _merge_kernel,
        grid=(t // tm,),
        in_specs=[
            pl.BlockSpec((tm, dm), lambda i: (i, 0)),
            pl.BlockSpec((tm, dm), lambda i: (i, 0)),
            pl.BlockSpec((tm, dm), lambda i: (i, 1)),
            pl.BlockSpec((tm, d), lambda i: (i, 1)),
            pl.BlockSpec((tm, d), lambda i: (i, 2)),
            pl.BlockSpec((tm, d), lambda i: (i, 3)),
            pl.BlockSpec((tm, d), lambda i: (i, 0)),
            pl.BlockSpec((1, ml, dm), lambda i: (i // per_b, 0, 0)),
            pl.BlockSpec((1, ml, dm), lambda i: (i // per_b, 0, 0)),
            pl.BlockSpec((dm, d), const),
            pl.BlockSpec((dm, d), const),
            pl.BlockSpec((dm, d), const),
            pl.BlockSpec((d, d), const),
            pl.BlockSpec((1, 3 * d), const),
            pl.BlockSpec((1, MEM_HD), const),
            pl.BlockSpec((1, d), const),
        ],
        out_specs=[
            pl.BlockSpec((tm, d), lambda i: (i, 0)),
            pl.BlockSpec((d, tm), lambda i: (0, i)),
        ],
        out_shape=[jax.ShapeDtypeStruct((t, d), F32), jax.ShapeDtypeStruct((d, t), BF16)],
        compiler_params=_cparams(("arbitrary",)),
        name="merge",
    )(ysb, yml, p2, p2, p2, p2, x2, kh, vh, wsb, wml, wmem, wout, b_gate, q_g, g_ffn)


_TAKEN = 2.0 ** 100


def _top16(x):
    vals = []
    for k in range(PEER_TOPK):
        m = jnp.max(x, axis=0, keepdims=True)
        vals.append(m)
        x = jnp.where(x == m, -_TAKEN * (1.0 + k / 16.0), x)
    rank = jnp.where(x < -0.5 * _TAKEN, x * (-16.0 / _TAKEN) - 16.0, float(PEER_TOPK))
    return vals, rank


_PAIRS = [(a, b) for a in range(PEER_TOPK) for b in range(PEER_TOPK) if (a + 1) * (b + 1) <= PEER_TOPK]
_N_CAND = -(-len(_PAIRS) // 8) * 8


def _route_kernel(hnt_ref, wq_ref, keys_ref, rank2_ref, e2_ref, n1_ref, e1_ref, qt_ref, cand_ref):
    tm = hnt_ref.shape[1]
    qt_ref[...] = _dot(wq_ref[...], hnt_ref[...]).astype(BF16)
    cand_ref[...] = jnp.full(cand_ref.shape, NEG, F32)

    def head(h, carry):
        r0 = pl.multiple_of(h * (2 * N_KEYS), 2 * N_KEYS)
        s1_all = _dot(keys_ref[2 * h], qt_ref[pl.ds(r0, N_KEYS), :])
        s2_all = _dot(keys_ref[2 * h + 1], qt_ref[pl.ds(r0 + N_KEYS, N_KEYS), :])
        for c in range(tm // LANES):
            ls = slice(c * LANES, (c + 1) * LANES)
            s1, s2 = s1_all[:, ls], s2_all[:, ls]
            v1, rank1 = _top16(s1)
            v2, rank2 = _top16(s2)
            sums = {}
            for n, (a, b) in enumerate(_PAIRS):
                sums[a, b] = v1[a] + v2[b]
                cand_ref[n:n + 1, ls] = sums[a, b]
            top, _ = _top16(cand_ref[:, ls])
            thr = top[PEER_TOPK - 1]
            zsum = jnp.ones_like(thr)
            for r in range(1, PEER_TOPK):
                zsum = zsum + jnp.exp(top[r] - top[0])
            n1 = jnp.zeros_like(s1)
            for a in range(PEER_TOPK):
                cnt = jnp.zeros_like(thr)
                for b in range(PEER_TOPK):
                    if (a, b) in sums:
                        cnt = cnt + jnp.where(sums[a, b] >= thr, 1.0, 0.0)
                n1 = jnp.where(rank1 == float(a), cnt, n1)
            n1_ref[h, :, ls] = n1
            e1_ref[h, :, ls] = jnp.exp(s1 - v1[0]) * (0.5 / zsum)
            rank2_ref[h, :, ls] = rank2.astype(BF16)
            e2_ref[h, :, ls] = jnp.exp(s2 - v2[0]).astype(BF16)
        return carry

    lax.fori_loop(0, PEER_HEADS, head, 0)


def _peer_route(hnt, wq_t, keys, tm):
    d, t = hnt.shape
    nq = wq_t.shape[0]
    blk = pl.BlockSpec((PEER_HEADS, N_KEYS, tm), lambda i: (0, 0, i))
    return pl.pallas_call(
        _route_kernel,
        grid=(t // tm,),
        in_specs=[
            pl.BlockSpec((d, tm), lambda i: (0, i)),
            pl.BlockSpec((nq, d), lambda i: (0, 0)),
            pl.BlockSpec((2 * PEER_HEADS, N_KEYS, N_KEYS), lambda i: (0, 0, 0)),
        ],
        out_specs=[blk] * 4,
        out_shape=[jax.ShapeDtypeStruct((PEER_HEADS, N_KEYS, t), BF16)] * 2
        + [jax.ShapeDtypeStruct((PEER_HEADS, N_KEYS, t), F32)] * 2,
        scratch_shapes=[pltpu.VMEM((nq, tm), BF16), pltpu.VMEM((_N_CAND, tm), F32)],
        compiler_params=_cparams(("arbitrary",)),
        name="peer_route",
    )(hnt, wq_t, keys)


def _dense_kernel(hnt_ref, u_ref, vprev_ref, vlast_ref, rank2_ref, e2_ref, n1_ref, e1_ref, h_ref, o_ref,
                  acc_ref, act_ref, w_ref, *, rows):
    e = pl.program_id(1)
    tm = hnt_ref.shape[1]
    cur = lax.rem(e, 2)

    @pl.when(e == 0)
    def _():
        acc_ref[...] = jnp.zeros_like(acc_ref)
        w_ref[1] = jnp.zeros(w_ref.shape[1:], w_ref.dtype)

    zero = jnp.zeros((), BF16)
    half = u_ref.shape[0] // 2

    def activations(m):
        ms = slice(m * half, (m + 1) * half)
        act_ref[ms, :] = _dot(u_ref[ms, :], hnt_ref[...])

    def fold(m):
        ms = slice(m * half, (m + 1) * half)
        acc_ref[ms, :] += _dot(vprev_ref[0, ms, :], w_ref[1 - cur])

    def gate_rows(lo, hi):
        for ii in range(lo, hi):
            i = e * rows + ii
            rs = slice(ii * N_KEYS, (ii + 1) * N_KEYS)
            n_rows = [n1_ref[h, pl.ds(i, 1), :] for h in range(PEER_HEADS)]
            e_rows = [e1_ref[h, pl.ds(i, 1), :] for h in range(PEER_HEADS)]
            for c in range(tm // GATE_W):
                ls = slice(c * GATE_W, (c + 1) * GATE_W)
                gate = None
                for h in range(PEER_HEADS):
                    nb = jnp.broadcast_to(n_rows[h][:, ls], (N_KEYS, GATE_W)).astype(BF16)
                    eb = jnp.broadcast_to(e_rows[h][:, ls], (N_KEYS, GATE_W)).astype(BF16)
                    term = jnp.where(rank2_ref[h, :, ls] < nb, e2_ref[h, :, ls] * eb, zero)
                    gate = term if gate is None else gate + term
                a = act_ref[rs, ls]
                w_ref[cur, rs, ls] = (a * (1.0 + lax.erf(a * (2.0 ** -0.5)))).astype(BF16) * gate

    q = rows // 4
    activations(0)
    activations(1)
    gate_rows(0, q)
    fold(0)
    gate_rows(q, 2 * q)
    fold(1)
    gate_rows(2 * q, rows)

    @pl.when(e == pl.num_programs(1) - 1)
    def _():
        acc = acc_ref[...] + _dot(vlast_ref[0], w_ref[cur])
        o_ref[...] = h_ref[...] + acc.T


def _peer_dense(hnt, u_bf, vt_blocks, rank2, e2, n1, e1, h1, tm):
    d, t = hnt.shape
    ne, _, eb = vt_blocks.shape
    rows = eb // N_KEYS
    by_key = by_row = pl.BlockSpec((PEER_HEADS, N_KEYS, tm), lambda i, e: (0, 0, i))
    return pl.pallas_call(
        functools.partial(_dense_kernel, rows=rows),
        grid=(t // tm, ne),
        in_specs=[
            pl.BlockSpec((d, tm), lambda i, e: (0, i)),
            pl.BlockSpec((eb, d), lambda i, e: (e, 0)),
            pl.BlockSpec((1, d, eb), lambda i, e: (jnp.maximum(e - 1, 0), 0, 0)),
            pl.BlockSpec((1, d, eb), lambda i, e: (ne - 1, 0, 0)),
            by_key, by_key, by_row, by_row,
            pl.BlockSpec((tm, d), lambda i, e: (i, 0)),
        ],
        out_specs=pl.BlockSpec((tm, d), lambda i, e: (i, 0)),
        out_shape=jax.ShapeDtypeStruct((t, d), F32),
        scratch_shapes=[
            pltpu.VMEM((d, tm), F32),
            pltpu.VMEM((eb, tm), F32),
            pltpu.VMEM((2, eb, tm), BF16),
        ],
        compiler_params=_cparams(("arbitrary", "arbitrary")),
        name="peer_dense",
    )(hnt, u_bf, vt_blocks, vt_blocks, rank2, e2, n1, e1, h1)


def _tiles(bsz, s):
    t = bsz * s
    return dict(
        proj=min(256, t),
        sb=min(256, s),
        chunk=min(256, s),
        merge=min(512, s),
        route=min(512, t),
        dense=min(512, t),
    )


def kernel(x, mem, g_mix, w_in, b_igate, b_fgate, conv_w, conv_b, ml_norm_g, mem_norm_g, w_mem_kv, q_norm_g, k_norm_g, w_sb_o, w_ml_o, w_mem_o, b_gate, w_out, g_ffn, w_peer_q, peer_k1, peer_k2, peer_u, peer_v):
    bsz, s, d = x.shape
    depth = g_mix.shape[0]
    tl = _tiles(bsz, s)
    sb_dim = SB_HEADS * SB_HD
    ml_dim = ML_HEADS * ML_HD
    mem_dim = MEM_HEADS * MEM_HD
    n1 = 3 * sb_dim + 3 * ml_dim
    n_if = 2 * ML_HEADS
    n2 = ml_dim + mem_dim + 3 * d

    h = x.reshape(bsz * s, d)
    for l in range(depth):
        w = w_in[l]
        w_if = jnp.pad(w[:, n1:n1 + n_if], ((0, 0), (0, LANES - n_if)))
        w_all = jnp.concatenate([w[:, :n1], w[:, n1 + n_if:], w_if], axis=1).astype(BF16)
        p1, p2, gates_if = _in_proj(h, g_mix[l][None], w_all, n1, n2, tl["proj"])

        y_sb = _stick_break(p1, bsz, s, tl["sb"], SB_PAIRS)

        bias_if = jnp.pad(jnp.concatenate([b_igate[l], b_fgate[l]]), (0, LANES - n_if))[None]
        y_ml = _mlstm(p1, p2, gates_if, conv_w[l], conv_b[l][None], bias_if, ml_norm_g[l][None],
                      bsz, s, tl["chunk"])

        kh, vh = _mem_kv(mem, mem_norm_g[l][None], w_mem_kv[l].astype(BF16), k_norm_g[l][None])

        h1, hnt = _merge(y_sb, y_ml, p2, h, kh, vh, w_sb_o[l].astype(BF16), w_ml_o[l].astype(BF16),
                         w_mem_o[l].astype(BF16), w_out[l].astype(BF16), b_gate[l][None],
                         q_norm_g[l][None], g_ffn[l][None], s, tl["merge"])

        keys = jnp.stack([peer_k1[l], peer_k2[l]], axis=1).reshape(2 * PEER_HEADS, N_KEYS, -1).astype(BF16)
        rank2, e2, n1, e1 = _peer_route(hnt, w_peer_q[l].T.astype(BF16), keys, tl["route"])
        vt_blocks = peer_v[l].astype(BF16).reshape(-1, DENSE_ROWS * N_KEYS, d).transpose(0, 2, 1)
        h = _peer_dense(hnt, peer_u[l].astype(BF16), vt_blocks, rank2, e2, n1, e1, h1, tl["dense"])
    return h.reshape(bsz, s, d)
```

```python
import functools

import jax
import jax.numpy as jnp
from jax import lax
from jax.experimental import pallas as pl
from jax.experimental.pallas import tpu as pltpu

F32 = jnp.float32
BF16 = jnp.bfloat16

EPS = 1e-6
SB_HEADS, SB_HD = 8, 64
SB_PAIRS = 4
ML_HEADS, ML_HD = 4, 128
MEM_HEADS, MEM_HD = 4, 128
MERGE_SUB = 2
CONV_W = 4
N_KEYS = 128
PEER_HEADS = 8
PEER_TOPK = 16
LANES = 128
DENSE_ROWS = 8
GATE_W = 256
VMEM_LIMIT = 56 << 20
NEG = -1e30
SB_DEAD = -104.0


def _cparams(sem):
    return pltpu.CompilerParams(dimension_semantics=sem, vmem_limit_bytes=VMEM_LIMIT)


def _rms(x, g):
    return x * lax.rsqrt(jnp.mean(x * x, axis=-1, keepdims=True) + EPS) * g


def _dot(a, b):
    return jnp.dot(a, b, preferred_element_type=F32)


def _dot_nt(a, b):
    return lax.dot_general(a, b, (((1,), (1,)), ((), ())), preferred_element_type=F32)


def _split_bf16(x):
    hi = x.astype(BF16)
    lo = (x - hi.astype(F32)).astype(BF16)
    return hi, lo


def _in_proj_kernel(x_ref, g_ref, w_ref, p1_ref, p2_ref, if_ref, *, n1, n2):
    xn = _rms(x_ref[...], g_ref[...]).astype(BF16)
    step = 1024
    for c in range(0, n1, step):
        p1_ref[:, c:c + step] = _dot(xn, w_ref[:, c:c + step]).astype(BF16)
    for c in range(0, n2, step):
        p2_ref[:, c:c + step] = _dot(xn, w_ref[:, n1 + c:n1 + c + step]).astype(BF16)
    if_ref[...] = _dot(xn, w_ref[:, n1 + n2:])


def _in_proj(x2, g, w_all, n1, n2, tm):
    t, d = x2.shape
    ncol = w_all.shape[1]
    return pl.pallas_call(
        functools.partial(_in_proj_kernel, n1=n1, n2=n2),
        grid=(t // tm,),
        in_specs=[
            pl.BlockSpec((tm, d), lambda i: (i, 0)),
            pl.BlockSpec((1, d), lambda i: (0, 0)),
            pl.BlockSpec((d, ncol), lambda i: (0, 0)),
        ],
        out_specs=[
            pl.BlockSpec((tm, n1), lambda i: (i, 0)),
            pl.BlockSpec((tm, n2), lambda i: (i, 0)),
            pl.BlockSpec((tm, LANES), lambda i: (i, 0)),
        ],
        out_shape=[
            jax.ShapeDtypeStruct((t, n1), BF16),
            jax.ShapeDtypeStruct((t, n2), BF16),
            jax.ShapeDtypeStruct((t, LANES), F32),
        ],
        compiler_params=_cparams(("arbitrary",)),
        name="in_proj",
    )(x2, g, w_all)


def _sb_kernel(*refs, tb, scale, pairs):
    q_refs, k_refs, v_refs, o_ref = refs[:pairs], refs[pairs:2 * pairs], refs[2 * pairs:3 * pairs], refs[3 * pairs]
    i = pl.program_id(2)
    heads = range(2 * pairs)
    lane = lax.broadcasted_iota(jnp.int32, (tb, LANES), 1)
    first = lane < SB_HD
    qs = []
    for g in range(pairs):
        q = q_refs[g][...] * scale
        zero = jnp.zeros_like(q)
        qs += [jnp.where(first, q, zero), jnp.where(first, zero, q)]
    row = lax.broadcasted_iota(jnp.int32, (tb, tb), 0)
    col = lax.broadcasted_iota(jnp.int32, (tb, tb), 1)
    below = row > col
    ones_below = jnp.where(below, 1.0, 0.0).astype(BF16)

    def block(j, cs, accs, diag):
        rows = pl.ds(pl.multiple_of(j * tb, tb), tb)
        ks = [k_refs[g][rows, :] for g in range(pairs)]
        vs = [v_refs[g][rows, :] for g in range(pairs)]
        zs = [_dot_nt(qs[n], ks[n // 2]) for n in heads]
        lks = []
        for n in heads:
            lk = -(jnp.maximum(zs[n], 0.0) + jnp.log(1.0 + jnp.exp(-jnp.abs(zs[n]))))
            lks.append(jnp.where(below, lk, 0.0) if diag else lk)
        sufs = []
        for n in heads:
            hi, lo = _split_bf16(lks[n])
            sufs.append(_dot(hi, ones_below) + _dot(lo, ones_below))
        outs = []
        for n in heads:
            a = jnp.exp(zs[n] + lks[n] + sufs[n] + cs[n])
            if diag:
                a = jnp.where(below, a, 0.0)
            outs.append(_dot(a.astype(BF16), vs[n // 2]))
        new_cs = tuple(cs[n] + jnp.sum(lks[n], axis=1, keepdims=True) for n in heads)
        new_accs = tuple(accs[g] + jnp.where(first, outs[2 * g], outs[2 * g + 1]) for g in range(pairs))
        return new_cs, new_accs

    c0 = jnp.zeros((tb, 1), F32)
    a0 = jnp.zeros((tb, LANES), F32)
    cs, accs = block(i, (c0,) * (2 * pairs), (a0,) * pairs, True)

    def alive(carry):
        j, cs, _ = carry
        return jnp.logical_and(j >= 0, jnp.max(functools.reduce(jnp.maximum, cs)) > SB_DEAD)

    def body(carry):
        j, cs, accs = carry
        cs, accs = block(j, cs, accs, False)
        return j - 1, cs, accs

    _, _, accs = lax.while_loop(alive, body, (i - 1, cs, accs))
    for g in range(pairs):
        o_ref[:, g * LANES:(g + 1) * LANES] = accs[g].astype(o_ref.dtype)


def _stick_break(p1, bsz, s, tb, pairs):
    t = bsz * s
    nq = s // tb
    npair = SB_HEADS * SB_HD // LANES
    ng = npair // pairs

    def query_col(g):
        return pl.BlockSpec((tb, LANES), lambda b, p, i: (b * nq + i, p * pairs + g))

    def seq_col(first_col, g):
        return pl.BlockSpec((s, LANES), lambda b, p, i: (b, first_col + p * pairs + g))

    groups = range(pairs)
    return pl.pallas_call(
        functools.partial(_sb_kernel, tb=tb, scale=SB_HD ** -0.5, pairs=pairs),
        grid=(bsz, ng, nq),
        in_specs=([query_col(g) for g in groups] + [seq_col(npair, g) for g in groups]
                  + [seq_col(2 * npair, g) for g in groups]),
        out_specs=pl.BlockSpec((tb, pairs * LANES), lambda b, p, i: (b * nq + i, p)),
        out_shape=jax.ShapeDtypeStruct((t, npair * LANES), BF16),
        compiler_params=_cparams(("arbitrary", "arbitrary", "arbitrary")),
        name="stick_break",
    )(*([p1] * (3 * pairs)))


def _mlstm_kernel(qr_ref, kr_ref, v_ref, qp_ref, kp_ref, if_ref, o_ref, cw_ref, cb_ref, bif_ref, ng_ref,
                  y_ref, qpad, kpad, c_st, n_st, m_st, *, L):
    ch = pl.program_id(1)
    dm = ML_HEADS * ML_HD

    @pl.when(ch == 0)
    def _():
        c_st[...] = jnp.zeros_like(c_st)
        n_st[...] = jnp.zeros_like(n_st)
        m_st[...] = jnp.zeros_like(m_st)

    keep_prev = jnp.where(ch == 0, 0.0, 1.0)
    convd = []
    for idx, (cur, prev, pad) in enumerate(((qr_ref, qp_ref, qpad), (kr_ref, kp_ref, kpad))):
        pad[0:8, :] = prev[...].astype(F32) * keep_prev
        pad[8:8 + L, :] = cur[...].astype(F32)
        y = cb_ref[:, idx * dm:(idx + 1) * dm]
        for tap in range(CONV_W):
            y = y + cw_ref[tap:tap + 1, idx * dm:(idx + 1) * dm] * pad[pl.ds(8 - (CONV_W - 1) + tap, L), :]
        convd.append(y * jax.nn.sigmoid(y))
    q_all = convd[0]
    k_all = convd[1] * (ML_HD ** -0.5)

    gates = if_ref[...] + bif_ref[...]
    lf = jnp.minimum(gates, 0.0) - jnp.log1p(jnp.exp(-jnp.abs(gates)))
    row = lax.broadcasted_iota(jnp.int32, (L, L), 0)
    col = lax.broadcasted_iota(jnp.int32, (L, L), 1)
    causal = col <= row
    tri = jnp.where(causal, 1.0, 0.0).astype(BF16)
    hi, lo = _split_bf16(lf)
    lo2 = (lf - hi.astype(F32) - lo.astype(F32)).astype(BF16)
    bcum = _dot(tri, hi) + _dot(tri, lo) + _dot(tri, lo2)
    lane = lax.broadcasted_iota(jnp.int32, (L, LANES), 1)
    packed = jnp.where(lane < ML_HEADS, gates, bcum)
    packed_t = packed.T

    hds = range(ML_HEADS)
    sl = [slice(h * ML_HD, (h + 1) * ML_HD) for h in hds]
    q = [q_all[:, sl[h]] for h in hds]
    k = [k_all[:, sl[h]] for h in hds]
    v = [v_ref[:, sl[h]] for h in hds]
    qb = [q[h].astype(BF16) for h in hds]
    ig_col = [gates[:, h:h + 1] for h in hds]
    bc_col = [bcum[:, ML_HEADS + h:ML_HEADS + h + 1] for h in hds]
    ig_row = [packed_t[h:h + 1, :] for h in hds]
    bc_row = [packed_t[ML_HEADS + h:ML_HEADS + h + 1, :] for h in hds]
    m_prev = [m_st[h][0:1, 0:1] for h in hds]

    qk = [_dot_nt(qb[h], k[h].astype(BF16)) for h in hds]
    qc = [_dot(qb[h], c_st[h].astype(BF16)) for h in hds]
    dmat = [jnp.where(causal, bc_col[h] - bc_row[h] + ig_row[h], NEG) for h in hds]
    m_inter = [bc_col[h] + m_prev[h] for h in hds]
    m_t = [jnp.maximum(m_inter[h], jnp.max(dmat[h], axis=1, keepdims=True)) for h in hds]
    w = [qk[h] * jnp.exp(dmat[h] - m_t[h]) for h in hds]
    inter = [jnp.exp(m_inter[h] - m_t[h]) for h in hds]
    wv = [_dot(w[h].astype(BF16), v[h]) for h in hds]

    b_last = [bc_col[h][L - 1:L, :] for h in hds]
    g_col = [b_last[h] - bc_col[h] + ig_col[h] for h in hds]
    m_new = [jnp.maximum(b_last[h] + m_prev[h], jnp.max(g_col[h], axis=0, keepdims=True)) for h in hds]
    kw = [k[h] * jnp.exp(g_col[h] - m_new[h]) for h in hds]
    decay = [jnp.exp(b_last[h] + m_prev[h] - m_new[h]) for h in hds]
    kwv = [_dot(kw[h].T.astype(BF16), v[h]) for h in hds]

    den = [jnp.sum(w[h], axis=1, keepdims=True) + inter[h] * jnp.sum(q[h] * n_st[h], axis=1, keepdims=True)
           for h in hds]
    hv = [(wv[h] + inter[h] * qc[h]) / jnp.maximum(jnp.abs(den[h]), jnp.exp(-m_t[h])) for h in hds]
    for h in hds:
        c_st[h] = decay[h] * c_st[h] + kwv[h]
        n_st[h] = decay[h] * n_st[h] + jnp.sum(kw[h], axis=0, keepdims=True)
        m_st[h] = jnp.broadcast_to(m_new[h], (8, LANES))
    for h in hds:
        hn = _rms(hv[h], ng_ref[...])
        y_ref[:, sl[h]] = (hn * jax.nn.sigmoid(o_ref[:, sl[h]].astype(F32))).astype(y_ref.dtype)


def _mlstm(p1, p2, gates_if, conv_w, conv_b, bias_if, norm_g, bsz, s, L):
    t = bsz * s
    nc = s // L
    dm = ML_HEADS * ML_HD
    lb = L // 8

    def prev_map(b, c):
        return (jnp.maximum((b * nc + c) * lb - 1, 0), 0)

    return pl.pallas_call(
        functools.partial(_mlstm_kernel, L=L),
        grid=(bsz, nc),
        in_specs=[
            pl.BlockSpec((L, dm), lambda b, c: (b * nc + c, 3)),
            pl.BlockSpec((L, dm), lambda b, c: (b * nc + c, 4)),
            pl.BlockSpec((L, dm), lambda b, c: (b * nc + c, 5)),
            pl.BlockSpec((8, dm), lambda b, c: (prev_map(b, c)[0], 3)),
            pl.BlockSpec((8, dm), lambda b, c: (prev_map(b, c)[0], 4)),
            pl.BlockSpec((L, LANES), lambda b, c: (b * nc + c, 0)),
            pl.BlockSpec((L, dm), lambda b, c: (b * nc + c, 0)),
            pl.BlockSpec((CONV_W, 2 * dm), lambda b, c: (0, 0)),
            pl.BlockSpec((1, 2 * dm), lambda b, c: (0, 0)),
            pl.BlockSpec((1, LANES), lambda b, c: (0, 0)),
            pl.BlockSpec((1, ML_HD), lambda b, c: (0, 0)),
        ],
        out_specs=pl.BlockSpec((L, dm), lambda b, c: (b * nc + c, 0)),
        out_shape=jax.ShapeDtypeStruct((t, dm), BF16),
        scratch_shapes=[
            pltpu.VMEM((L + 8, dm), F32),
            pltpu.VMEM((L + 8, dm), F32),
            pltpu.VMEM((ML_HEADS, ML_HD, ML_HD), F32),
            pltpu.VMEM((ML_HEADS, 1, ML_HD), F32),
            pltpu.VMEM((ML_HEADS, 8, LANES), F32),
        ],
        compiler_params=_cparams(("arbitrary", "arbitrary")),
        name="mlstm",
    )(p1, p1, p1, p1, p1, gates_if, p2, conv_w, conv_b, bias_if, norm_g)


def _mem_kv_kernel(mem_ref, g_ref, w_ref, kg_ref, k_ref, v_ref):
    memn = _rms(mem_ref[0], g_ref[...]).astype(BF16)
    kv = _dot(memn, w_ref[...])
    dm = MEM_HEADS * MEM_HD
    for h in range(MEM_HEADS):
        sl = slice(h * MEM_HD, (h + 1) * MEM_HD)
        k_ref[0, :, sl] = _rms(kv[:, sl], kg_ref[...]).astype(BF16)
    v_ref[0] = kv[:, dm:].astype(BF16)


def _mem_kv(mem, g, w_kv, k_g):
    bsz, ml, d = mem.shape
    dm = MEM_HEADS * MEM_HD
    return pl.pallas_call(
        _mem_kv_kernel,
        grid=(bsz,),
        in_specs=[
            pl.BlockSpec((1, ml, d), lambda b: (b, 0, 0)),
            pl.BlockSpec((1, d), lambda b: (0, 0)),
            pl.BlockSpec((d, 2 * dm), lambda b: (0, 0)),
            pl.BlockSpec((1, MEM_HD), lambda b: (0, 0)),
        ],
        out_specs=[
            pl.BlockSpec((1, ml, dm), lambda b: (b, 0, 0)),
            pl.BlockSpec((1, ml, dm), lambda b: (b, 0, 0)),
        ],
        out_shape=[jax.ShapeDtypeStruct((bsz, ml, dm), BF16)] * 2,
        compiler_params=_cparams(("arbitrary",)),
        name="mem_kv",
    )(mem, g, w_kv, k_g)


def _merge_kernel(ysb_ref, yml_ref, mq_ref, g0_ref, g1_ref, g2_ref, x_ref, kh_ref, vh_ref,
                  wsb_ref, wml_ref, wmem_ref, wout_ref, bg_ref, qg_ref, gf_ref, h_ref, hnt_ref):
    tm, d = x_ref.shape
    sub = tm // MERGE_SUB
    rws = [slice(r * sub, (r + 1) * sub) for r in range(MERGE_SUB)]
    hds = [slice(h * MEM_HD, (h + 1) * MEM_HD) for h in range(MEM_HEADS)]
    qh = [[_rms(mq_ref[rs, hs].astype(F32), qg_ref[...]).astype(BF16) for hs in hds] for rs in rws]
    sc = [[_dot_nt(q, kh_ref[0, :, hs]) * (MEM_HD ** -0.5) for q, hs in zip(qr, hds)] for qr in qh]
    p = [[jnp.exp(s - jnp.max(s, axis=1, keepdims=True)) for s in sr] for sr in sc]
    p = [[(x / jnp.sum(x, axis=1, keepdims=True)).astype(BF16) for x in pr] for pr in p]
    att = [[_dot(x, vh_ref[0, :, hs]) for x, hs in zip(pr, hds)] for pr in p]
    ymem = [jnp.concatenate(ar, axis=1).astype(BF16) for ar in att]
    dsb = [_dot(ysb_ref[rs, :], wsb_ref[...]) for rs in rws]
    dml = [_dot(yml_ref[rs, :], wml_ref[...]) for rs in rws]
    dmem = [_dot(y, wmem_ref[...]) for y in ymem]

    def gate(ref, rs, idx):
        return jax.nn.sigmoid(ref[rs, :].astype(F32) + bg_ref[:, idx * d:(idx + 1) * d])

    merged = [(gate(g0_ref, rs, 0) * a + gate(g1_ref, rs, 1) * b + gate(g2_ref, rs, 2) * c).astype(BF16)
              for rs, a, b, c in zip(rws, dsb, dml, dmem)]
    h1 = [x_ref[rs, :] + _dot(m, wout_ref[...]) for rs, m in zip(rws, merged)]
    for rs, hh in zip(rws, h1):
        h_ref[rs, :] = hh
        hnt_ref[:, rs] = _rms(hh, gf_ref[...]).T.astype(BF16)


def _merge(ysb, yml, p2, x2, kh, vh, wsb, wml, wmem, wout, b_gate, q_g, g_ffn, s, tm):
    t, d = x2.shape
    dm = MEM_HEADS * MEM_HD
    ml = kh.shape[1]
    per_b = s // tm
    const = lambda i: (0, 0)
    return pl.pallas_call(
        _merge_kernel,
        grid=(t // tm,),
        in_specs=[
            pl.BlockSpec((tm, dm), lambda i: (i, 0)),
            pl.BlockSpec((tm, dm), lambda i: (i, 0)),
            pl.BlockSpec((tm, dm), lambda i: (i, 1)),
            pl.BlockSpec((tm, d), lambda i: (i, 1)),
            pl.BlockSpec((tm, d), lambda i: (i, 2)),
            pl.BlockSpec((tm, d), lambda i: (i, 3)),
            pl.BlockSpec((tm, d), lambda i: (i, 0)),
            pl.BlockSpec((1, ml, dm), lambda i: (i // per_b, 0, 0)),
            pl.BlockSpec((1, ml, dm), lambda i: (i // per_b, 0, 0)),
            pl.BlockSpec((dm, d), const),
            pl.BlockSpec((dm, d), const),
            pl.BlockSpec((dm, d), const),
            pl.BlockSpec((d, d), const),
            pl.BlockSpec((1, 3 * d), const),
            pl.BlockSpec((1, MEM_HD), const),
            pl.BlockSpec((1, d), const),
        ],
        out_specs=[
            pl.BlockSpec((tm, d), lambda i: (i, 0)),
            pl.BlockSpec((d, tm), lambda i: (0, i)),
        ],
        out_shape=[jax.ShapeDtypeStruct((t, d), F32), jax.ShapeDtypeStruct((d, t), BF16)],
        compiler_params=_cparams(("arbitrary",)),
        name="merge",
    )(ysb, yml, p2, p2, p2, p2, x2, kh, vh, wsb, wml, wmem, wout, b_gate, q_g, g_ffn)


_TAKEN = 2.0 ** 100


def _top16(x):
    vals = []
    for k in range(PEER_TOPK):
        m = jnp.max(x, axis=0, keepdims=True)
        vals.append(m)
        x = jnp.where(x == m, -_TAKEN * (1.0 + k / 16.0), x)
    rank = jnp.where(x < -0.5 * _TAKEN, x * (-16.0 / _TAKEN) - 16.0, float(PEER_TOPK))
    return vals, rank


_PAIRS = [(a, b) for a in range(PEER_TOPK) for b in range(PEER_TOPK) if (a + 1) * (b + 1) <= PEER_TOPK]
_N_CAND = -(-len(_PAIRS) // 8) * 8


def _route_kernel(hnt_ref, wq_ref, keys_ref, rank2_ref, e2_ref, n1_ref, e1_ref, qt_ref, cand_ref):
    tm = hnt_ref.shape[1]
    qt_ref[...] = _dot(wq_ref[...], hnt_ref[...]).astype(BF16)
    cand_ref[...] = jnp.full(cand_ref.shape, NEG, F32)

    def head(h, carry):
        r0 = pl.multiple_of(h * (2 * N_KEYS), 2 * N_KEYS)
        s1_all = _dot(keys_ref[2 * h], qt_ref[pl.ds(r0, N_KEYS), :])
        s2_all = _dot(keys_ref[2 * h + 1], qt_ref[pl.ds(r0 + N_KEYS, N_KEYS), :])
        for c in range(tm // LANES):
            ls = slice(c * LANES, (c + 1) * LANES)
            s1, s2 = s1_all[:, ls], s2_all[:, ls]
            v1, rank1 = _top16(s1)
            v2, rank2 = _top16(s2)
            sums = {}
            for n, (a, b) in enumerate(_PAIRS):
                sums[a, b] = v1[a] + v2[b]
                cand_ref[n:n + 1, ls] = sums[a, b]
            top, _ = _top16(cand_ref[:, ls])
            thr = top[PEER_TOPK - 1]
            zsum = jnp.ones_like(thr)
            for r in range(1, PEER_TOPK):
                zsum = zsum + jnp.exp(top[r] - top[0])
            n1 = jnp.zeros_like(s1)
            for a in range(PEER_TOPK):
                cnt = jnp.zeros_like(thr)
                for b in range(PEER_TOPK):
                    if (a, b) in sums:
                        cnt = cnt + jnp.where(sums[a, b] >= thr, 1.0, 0.0)
                n1 = jnp.where(rank1 == float(a), cnt, n1)
            n1_ref[h, :, ls] = n1
            e1_ref[h, :, ls] = jnp.exp(s1 - v1[0]) * (0.5 / zsum)
            rank2_ref[h, :, ls] = rank2.astype(BF16)
            e2_ref[h, :, ls] = jnp.exp(s2 - v2[0]).astype(BF16)
        return carry

    lax.fori_loop(0, PEER_HEADS, head, 0)


def _peer_route(hnt, wq_t, keys, tm):
    d, t = hnt.shape
    nq = wq_t.shape[0]
    blk = pl.BlockSpec((PEER_HEADS, N_KEYS, tm), lambda i: (0, 0, i))
    return pl.pallas_call(
        _route_kernel,
        grid=(t // tm,),
        in_specs=[
            pl.BlockSpec((d, tm), lambda i: (0, i)),
            pl.BlockSpec((nq, d), lambda i: (0, 0)),
            pl.BlockSpec((2 * PEER_HEADS, N_KEYS, N_KEYS), lambda i: (0, 0, 0)),
        ],
        out_specs=[blk] * 4,
        out_shape=[jax.ShapeDtypeStruct((PEER_HEADS, N_KEYS, t), BF16)] * 2
        + [jax.ShapeDtypeStruct((PEER_HEADS, N_KEYS, t), F32)] * 2,
        scratch_shapes=[pltpu.VMEM((nq, tm), BF16), pltpu.VMEM((_N_CAND, tm), F32)],
        compiler_params=_cparams(("arbitrary",)),
        name="peer_route",
    )(hnt, wq_t, keys)


def _dense_kernel(hnt_ref, u_ref, vprev_ref, vlast_ref, rank2_ref, e2_ref, n1_ref, e1_ref, h_ref, o_ref,
                  acc_ref, act_ref, w_ref, *, rows):
    e = pl.program_id(1)
    tm = hnt_ref.shape[1]
    cur = lax.rem(e, 2)

    @pl.when(e == 0)
    def _():
        acc_ref[...] = jnp.zeros_like(acc_ref)
        w_ref[1] = jnp.zeros(w_ref.shape[1:], w_ref.dtype)

    zero = jnp.zeros((), BF16)
    half = u_ref.shape[0] // 2

    def activations(m):
        ms = slice(m * half, (m + 1) * half)
        act_ref[ms, :] = _dot(u_ref[ms, :], hnt_ref[...])

    def fold(m):
        ms = slice(m * half, (m + 1) * half)
        acc_ref[ms, :] += _dot(vprev_ref[0, ms, :], w_ref[1 - cur])

    def gate_rows(lo, hi):
        for ii in range(lo, hi):
            i = e * rows + ii
            rs = slice(ii * N_KEYS, (ii + 1) * N_KEYS)
            n_rows = [n1_ref[h, pl.ds(i, 1), :] for h in range(PEER_HEADS)]
            e_rows = [e1_ref[h, pl.ds(i, 1), :] for h in range(PEER_HEADS)]
            for c in range(tm // GATE_W):
                ls = slice(c * GATE_W, (c + 1) * GATE_W)
                gate = None
                for h in range(PEER_HEADS):
                    nb = jnp.broadcast_to(n_rows[h][:, ls], (N_KEYS, GATE_W)).astype(BF16)
                    eb = jnp.broadcast_to(e_rows[h][:, ls], (N_KEYS, GATE_W)).astype(BF16)
                    term = jnp.where(rank2_ref[h, :, ls] < nb, e2_ref[h, :, ls] * eb, zero)
                    gate = term if gate is None else gate + term
                a = act_ref[rs, ls]
                w_ref[cur, rs, ls] = (a * (1.0 + lax.erf(a * (2.0 ** -0.5)))).astype(BF16) * gate

    q = rows // 4
    activations(0)
    activations(1)
    gate_rows(0, q)
    fold(0)
    gate_rows(q, 2 * q)
    fold(1)
    gate_rows(2 * q, rows)

    @pl.when(e == pl.num_programs(1) - 1)
    def _():
        acc = acc_ref[...] + _dot(vlast_ref[0], w_ref[cur])
        o_ref[...] = h_ref[...] + acc.T


def _peer_dense(hnt, u_bf, vt_blocks, rank2, e2, n1, e1, h1, tm):
    d, t = hnt.shape
    ne, _, eb = vt_blocks.shape
    rows = eb // N_KEYS
    by_key = by_row = pl.BlockSpec((PEER_HEADS, N_KEYS, tm), lambda i, e: (0, 0, i))
    return pl.pallas_call(
        functools.partial(_dense_kernel, rows=rows),
        grid=(t // tm, ne),
        in_specs=[
            pl.BlockSpec((d, tm), lambda i, e: (0, i)),
            pl.BlockSpec((eb, d), lambda i, e: (e, 0)),
            pl.BlockSpec((1, d, eb), lambda i, e: (jnp.maximum(e - 1, 0), 0, 0)),
            pl.BlockSpec((1, d, eb), lambda i, e: (ne - 1, 0, 0)),
            by_key, by_key, by_row, by_row,
            pl.BlockSpec((tm, d), lambda i, e: (i, 0)),
        ],
        out_specs=pl.BlockSpec((tm, d), lambda i, e: (i, 0)),
        out_shape=jax.ShapeDtypeStruct((t, d), F32),
        scratch_shapes=[
            pltpu.VMEM((d, tm), F32),
            pltpu.VMEM((eb, tm), F32),
            pltpu.VMEM((2, eb, tm), BF16),
        ],
        compiler_params=_cparams(("arbitrary", "arbitrary")),
        name="peer_dense",
    )(hnt, u_bf, vt_blocks, vt_blocks, rank2, e2, n1, e1, h1)


def _tiles(bsz, s):
    t = bsz * s
    return dict(
        proj=min(256, t),
        sb=min(256, s),
        chunk=min(256, s),
        merge=min(512, s),
        route=min(512, t),
        dense=min(512, t),
    )


def kernel(x, mem, g_mix, w_in, b_igate, b_fgate, conv_w, conv_b, ml_norm_g, mem_norm_g, w_mem_kv, q_norm_g, k_norm_g, w_sb_o, w_ml_o, w_mem_o, b_gate, w_out, g_ffn, w_peer_q, peer_k1, peer_k2, peer_u, peer_v):
    bsz, s, d = x.shape
    depth = g_mix.shape[0]
    tl = _tiles(bsz, s)
    sb_dim = SB_HEADS * SB_HD
    ml_dim = ML_HEADS * ML_HD
    mem_dim = MEM_HEADS * MEM_HD
    n1 = 3 * sb_dim + 3 * ml_dim
    n_if = 2 * ML_HEADS
    n2 = ml_dim + mem_dim + 3 * d

    h = x.reshape(bsz * s, d)
    for l in range(depth):
        w = w_in[l].astype(BF16)
        w_if = jnp.pad(w[:, n1:n1 + n_if], ((0, 0), (0, LANES - n_if)))
        w_all = jnp.concatenate([w[:, :n1], w[:, n1 + n_if:], w_if], axis=1)
        p1, p2, gates_if = _in_proj(h, g_mix[l][None], w_all, n1, n2, tl["proj"])

        y_sb = _stick_break(p1, bsz, s, tl["sb"], SB_PAIRS)

        bias_if = jnp.pad(jnp.concatenate([b_igate[l], b_fgate[l]]), (0, LANES - n_if))[None]
        y_ml = _mlstm(p1, p2, gates_if, conv_w[l], conv_b[l][None], bias_if, ml_norm_g[l][None],
                      bsz, s, tl["chunk"])

        kh, vh = _mem_kv(mem, mem_norm_g[l][None], w_mem_kv[l].astype(BF16), k_norm_g[l][None])

        h1, hnt = _merge(y_sb, y_ml, p2, h, kh, vh, w_sb_o[l].astype(BF16), w_ml_o[l].astype(BF16),
                         w_mem_o[l].astype(BF16), w_out[l].astype(BF16), b_gate[l][None],
                         q_norm_g[l][None], g_ffn[l][None], s, tl["merge"])

        keys = jnp.stack([peer_k1[l], peer_k2[l]], axis=1).reshape(2 * PEER_HEADS, N_KEYS, -1).astype(BF16)
        rank2, e2, n1, e1 = _peer_route(hnt, w_peer_q[l].T.astype(BF16), keys, tl["route"])
        vt_blocks = peer_v[l].reshape(-1, DENSE_ROWS * N_KEYS, d).transpose(0, 2, 1).astype(BF16)
        h = _peer_dense(hnt, peer_u[l].astype(BF16), vt_blocks, rank2, e2, n1, e1, h1, tl["dense"])
    return h.reshape(bsz, s, d)
```
